```python
import jax
import jax.numpy as jnp
from jax import lax
import numpy as np

D_MODEL = 1024
BATCH = 2
SEQ = 8192
DEPTH = 4
DEC_BATCH = 128
DEC_SEQ = 1
PAST_LEN = 2048
PAGE_SIZE = 128

CHUNK = 128
D_GMLP = 1024
GMLP_GROUPS = 8
GMLP_GW = D_GMLP // GMLP_GROUPS
N_HEADS = 16
KV_HEADS = 2
Q_PER_KV = N_HEADS // KV_HEADS
HEAD_DIM = 64
D_ATTN = N_HEADS * HEAD_DIM
D_KV = KV_HEADS * HEAD_DIM
CMP_BLOCK = 32
CMP_STRIDE = 16
CMP_RATIO = CMP_BLOCK // CMP_STRIDE
CMP_HIDDEN = 2 * HEAD_DIM
SEL_BLOCK = 64
N_SELECT = 16
WINDOW = 512
Q_BLOCK = 128
FORCE_BONUS = 1e4
N_KV_SLOTS = 4
N_EXPERTS = 16
N_EXPERT_GROUPS = 4
EXPERTS_PER_GROUP = N_EXPERTS // N_EXPERT_GROUPS
TOP_K = 2
D_EXPERT = 512
PROJ_SIZES = (D_GMLP, D_GMLP, D_ATTN) + (D_KV,) * 6 + (3 * N_HEADS, D_MODEL, D_MODEL)
D_IN = sum(PROJ_SIZES)

kernel_name = "gmlp_nsa_gated_hybrid_moe_step"


def layer_norm(x, g, b, eps=1e-5):
    xf = x.astype(jnp.float32)
    mu = jnp.mean(xf, -1, keepdims=True)
    var = jnp.mean(jnp.square(xf - mu), -1, keepdims=True)
    return ((xf - mu) * lax.rsqrt(var + eps) * g.astype(jnp.float32) + b.astype(jnp.float32)).astype(x.dtype)


def masked_softmax(s, mask):
    s = jnp.where(mask, s.astype(jnp.float32), -1e30)
    m = jnp.max(s, -1, keepdims=True)
    p = jnp.where(mask, jnp.exp(s - m), 0.0)
    return p / jnp.maximum(jnp.sum(p, -1, keepdims=True), 1e-30)


def split_projection(x, w):
    cuts = [int(c) for c in np.cumsum(PROJ_SIZES)[:-1]]
    return jnp.split(x @ w, cuts, axis=-1)


def heads_kv(t):
    return t.reshape(t.shape[:2] + (KV_HEADS, HEAD_DIM))


def heads_q(t):
    return t.reshape(t.shape[:2] + (KV_HEADS, Q_PER_KV, HEAD_DIM))


def nsa_gates(g):
    return jax.nn.sigmoid(g).reshape(g.shape[:2] + (KV_HEADS, Q_PER_KV, 3))


def gmlp_mix(u, v, w_s, b_s):
    B, T, _ = v.shape
    n_chunks = -(-T // CHUNK)
    vp = jnp.pad(v, ((0, 0), (0, n_chunks * CHUNK - T), (0, 0)))
    vp = vp.reshape(B, n_chunks, CHUNK, GMLP_GROUPS, GMLP_GW)
    causal = jnp.tril(jnp.ones((CHUNK, CHUNK), bool))
    w = jnp.where(causal[None], w_s, 0.0)
    mixed = jnp.einsum('gts,bcsgd->bctgd', w, vp) + b_s.T[None, None, :, :, None]
    return u * mixed.reshape(B, n_chunks * CHUNK, D_GMLP)[:, :T]


def gmlp_branch(u, v, ln_g, ln_b, w_s, b_s):
    vn = layer_norm(jax.nn.gelu(v), ln_g, ln_b)
    return gmlp_mix(jax.nn.gelu(u), vn, w_s, b_s), vn


def compress(k, pos, w1, w2):
    B, T = k.shape[:2]
    n_sub = T // CMP_STRIDE
    n_cmp = n_sub - CMP_RATIO + 1
    sub = k[:, :n_sub * CMP_STRIDE].reshape(B, n_sub, CMP_STRIDE, KV_HEADS, HEAD_DIM)
    w1b = w1.reshape(CMP_BLOCK, HEAD_DIM, CMP_HIDDEN)
    w1r = w1.reshape(CMP_RATIO, CMP_STRIDE, HEAD_DIM, CMP_HIDDEN)
    hid = jnp.einsum('sd,sdf->f', pos, w1b)
    for r in range(CMP_RATIO):
        hid = hid + jnp.einsum('bnshd,sdf->bnhf', sub[:, r:r + n_cmp], w1r[r])
    return jax.nn.gelu(hid) @ w2


def sel_blocks(k, n_sel):
    B, T = k.shape[:2]
    kp = jnp.pad(k, ((0, 0), (0, n_sel * SEL_BLOCK - T), (0, 0), (0, 0)))
    return kp.reshape(B, n_sel, SEL_BLOCK, KV_HEADS, HEAD_DIM).transpose(0, 3, 1, 2, 4)


def block_overlap(n_cmp, n_sel):
    c0 = np.arange(n_cmp) * CMP_STRIDE
    s0 = np.arange(n_sel) * SEL_BLOCK
    ov = (c0[:, None] < s0[None, :] + SEL_BLOCK) & (c0[:, None] + CMP_BLOCK > s0[None, :])
    return jnp.asarray(ov.astype(np.float32))


def nsa_context(k_cm, v_cm, k_sl, v_sl, pos_c, w1_c, w2_c):
    T = k_cm.shape[1]
    kc = compress(k_cm, pos_c[0], w1_c[0], w2_c[0])
    vc = compress(v_cm, pos_c[1], w1_c[1], w2_c[1])
    n_sel = -(-T // SEL_BLOCK)
    return (kc, vc, sel_blocks(k_sl, n_sel), sel_blocks(v_sl, n_sel), block_overlap(kc.shape[1], n_sel))


def nsa_block(q, qpos, gates, ctx, kw, vw, wpos):
    kc, vc, kb, vb, overlap = ctx
    B, Q = q.shape[:2]
    n_cmp, n_sel = kc.shape[1], kb.shape[2]
    scale = HEAD_DIM ** -0.5
    cmp_end = jnp.arange(n_cmp) * CMP_STRIDE + (CMP_BLOCK - 1)
    m_c = (cmp_end[None, :] <= qpos[:, None])[None, :, None, None, :]
    p_c = masked_softmax(jnp.einsum('bqkgd,bnkd->bqkgn', q, kc) * scale, m_c)
    o_c = jnp.einsum('bqkgn,bnkd->bqkgd', p_c.astype(vc.dtype), vc)
    imp = jnp.einsum('bqkn,nj->bqkj', jnp.sum(p_c, 3), overlap)
    blk = jnp.arange(n_sel)[None, :]
    cur = (qpos // SEL_BLOCK)[:, None]
    forced = (blk == 0) | (blk == cur) | (blk == cur - 1)
    valid = blk * SEL_BLOCK <= qpos[:, None]
    score = jnp.where(valid[None, :, None, :], imp + jnp.where(forced, FORCE_BONUS, 0.0)[None, :, None, :], -1e30)
    n_top = min(N_SELECT, n_sel)
    _, idx = lax.top_k(score, n_top)
    idx_t = idx.transpose(0, 2, 1, 3).reshape(B, KV_HEADS, Q * n_top)
    gather = jax.vmap(jax.vmap(lambda blocks, i: blocks[i]))
    ks = gather(kb, idx_t).reshape(B, KV_HEADS, Q, n_top * SEL_BLOCK, HEAD_DIM)
    vs = gather(vb, idx_t).reshape(B, KV_HEADS, Q, n_top * SEL_BLOCK, HEAD_DIM)
    tok = (idx[..., None] * SEL_BLOCK + jnp.arange(SEL_BLOCK)).reshape(B, Q, KV_HEADS, 1, n_top * SEL_BLOCK)
    m_s = tok <= qpos[None, :, None, None, None]
    p_s = masked_softmax(jnp.einsum('bqkgd,bkqmd->bqkgm', q, ks) * scale, m_s)
    o_s = jnp.einsum('bqkgm,bkqmd->bqkgd', p_s.astype(vs.dtype), vs)
    rel = qpos[:, None] - wpos[None, :]
    m_w = ((rel >= 0) & (rel < WINDOW) & (wpos[None, :] >= 0))[None, :, None, None, :]
    p_w = masked_softmax(jnp.einsum('bqkgd,bwkd->bqkgw', q, kw) * scale, m_w)
    o_w = jnp.einsum('bqkgw,bwkd->bqkgd', p_w.astype(vw.dtype), vw)
    return gates[..., 0:1] * o_c + gates[..., 1:2] * o_s + gates[..., 2:3] * o_w


def nsa_prompt(q, gates, ctx, k_wn, v_wn):
    B, T = q.shape[:2]
    pad = ((0, 0), (WINDOW, 0), (0, 0), (0, 0))
    kwp, vwp = jnp.pad(k_wn, pad), jnp.pad(v_wn, pad)
    span = WINDOW + Q_BLOCK

    def one_block(i):
        s = i * Q_BLOCK
        qb = lax.dynamic_slice_in_dim(q, s, Q_BLOCK, axis=1)
        gb = lax.dynamic_slice_in_dim(gates, s, Q_BLOCK, axis=1)
        kwb = lax.dynamic_slice_in_dim(kwp, s, span, axis=1)
        vwb = lax.dynamic_slice_in_dim(vwp, s, span, axis=1)
        return nsa_block(qb, s + jnp.arange(Q_BLOCK), gb, ctx, kwb, vwb, s - WINDOW + jnp.arange(span))

    o = lax.map(one_block, jnp.arange(T // Q_BLOCK))
    return jnp.moveaxis(o, 0, 1).reshape(B, T, D_ATTN)


def merge_branches(a, o, g_a, g_b, w_pa, w_pb, w_o):
    y = jax.nn.sigmoid(g_a) * (a @ w_pa) + jax.nn.sigmoid(g_b) * (o @ w_pb)
    return y @ w_o


def moe_ffn(x, w_router, b_router, w_gate, w_up, w_down):
    shape = x.shape
    xt = x.reshape(-1, D_MODEL)
    n_tok = xt.shape[0]
    aff = jax.nn.sigmoid((xt @ w_router).astype(jnp.float32))
    grp = (aff + b_router.astype(jnp.float32)).reshape(n_tok, N_EXPERT_GROUPS, EXPERTS_PER_GROUP)
    g_sel = jnp.argmax(jnp.sum(lax.top_k(grp, TOP_K)[0], -1), -1)
    in_grp = jnp.take_along_axis(grp, g_sel[:, None, None], axis=1)[:, 0]
    _, loc = lax.top_k(in_grp, TOP_K)
    eid = g_sel[:, None] * EXPERTS_PER_GROUP + loc
    wts = jnp.take_along_axis(aff, eid, axis=1)
    wts = wts / jnp.sum(wts, -1, keepdims=True)
    flat_e = eid.reshape(-1)
    order = jnp.argsort(flat_e)
    tok = order // TOP_K
    xs = xt[tok]
    sizes = jnp.bincount(flat_e, length=N_EXPERTS).astype(jnp.int32)
    h = jax.nn.silu(lax.ragged_dot(xs, w_gate, sizes)) * lax.ragged_dot(xs, w_up, sizes)
    out = lax.ragged_dot(h, w_down, sizes) * wts.reshape(-1)[order][:, None].astype(xt.dtype)
    return jax.ops.segment_sum(out, tok, num_segments=n_tok).reshape(shape)


def setup_inputs(seed: int = 0) -> dict:
    key = jax.random.key(seed)
    ks = jax.random.split(key, 32)
    beta = (8.0 * DEPTH) ** -0.25
    n_pages = PAST_LEN // PAGE_SIZE
    n_phys = (5 * DEC_BATCH * n_pages) // 4
    win_buf = min(WINDOW, PAST_LEN)

    def nrm(k, shape, scale):
        return jax.random.normal(k, shape, jnp.float32) * scale

    bounds = np.cumsum((0,) + PROJ_SIZES)
    col_scale = np.ones((D_IN,), np.float32)
    for i in (4, 6, 8):
        col_scale[bounds[i]:bounds[i + 1]] = beta
    perm = jax.random.permutation(ks[3], n_phys)
    return {
        'x_prompt': nrm(ks[0], (BATCH, SEQ, D_MODEL), 1.0),
        'x_sample': nrm(ks[1], (DEC_BATCH, DEC_SEQ, D_MODEL), 1.0),
        'cache_kv': nrm(ks[2], (n_phys, DEPTH, PAGE_SIZE, N_KV_SLOTS, KV_HEADS, HEAD_DIM), 1.0),
        'cache_win': nrm(ks[4], (DEC_BATCH, DEPTH, win_buf, 2, KV_HEADS, HEAD_DIM), 1.0),
        'page_table': perm[:DEC_BATCH * n_pages].reshape(DEC_BATCH, n_pages).astype(jnp.int32),
        'w_in': nrm(ks[5], (DEPTH, D_MODEL, D_IN), D_MODEL ** -0.5) * jnp.asarray(col_scale),
        'gmlp_ln_g': 1.0 + nrm(ks[6], (DEPTH, D_GMLP), 0.02),
        'gmlp_ln_b': nrm(ks[7], (DEPTH, D_GMLP), 0.02),
        'gmlp_ws': nrm(ks[8], (DEPTH, GMLP_GROUPS, CHUNK, CHUNK), CHUNK ** -0.5),
        'gmlp_bs': 1.0 + nrm(ks[9], (DEPTH, GMLP_GROUPS, CHUNK), 0.02),
        'cmp_pos': nrm(ks[10], (DEPTH, 2, CMP_BLOCK, HEAD_DIM), 0.1),
        'cmp_w1': nrm(ks[11], (DEPTH, 2, CMP_BLOCK * HEAD_DIM, CMP_HIDDEN), (CMP_BLOCK * HEAD_DIM) ** -0.5),
        'cmp_w2': nrm(ks[12], (DEPTH, 2, CMP_HIDDEN, HEAD_DIM), CMP_HIDDEN ** -0.5),
        'w_branch_a': nrm(ks[13], (DEPTH, D_GMLP, D_MODEL), beta * D_GMLP ** -0.5),
        'w_branch_b': nrm(ks[14], (DEPTH, D_ATTN, D_MODEL), beta * D_ATTN ** -0.5),
        'w_out': nrm(ks[15], (DEPTH, D_MODEL, D_MODEL), beta * D_MODEL ** -0.5),
        'ln1_g': 1.0 + nrm(ks[16], (DEPTH, D_MODEL), 0.02),
        'ln1_b': nrm(ks[17], (DEPTH, D_MODEL), 0.02),
        'ln2_g': 1.0 + nrm(ks[18], (DEPTH, D_MODEL), 0.02),
        'ln2_b': nrm(ks[19], (DEPTH, D_MODEL), 0.02),
        'w_router': nrm(ks[20], (D_MODEL, N_EXPERTS), D_MODEL ** -0.5),
        'b_router': nrm(ks[21], (N_EXPERTS,), 0.01),
        'w_gate': nrm(ks[22], (DEPTH, N_EXPERTS, D_MODEL, D_EXPERT), D_MODEL ** -0.5),
        'w_up': nrm(ks[23], (DEPTH, N_EXPERTS, D_MODEL, D_EXPERT), D_MODEL ** -0.5),
        'w_down': nrm(ks[24], (DEPTH, N_EXPERTS, D_EXPERT, D_MODEL), beta * D_EXPERT ** -0.5),
    }


def reference(x_prompt, x_sample, cache_kv, cache_win, page_table, w_in, gmlp_ln_g, gmlp_ln_b, gmlp_ws, gmlp_bs,
              cmp_pos, cmp_w1, cmp_w2, w_branch_a, w_branch_b, w_out, ln1_g, ln1_b, ln2_g, ln2_b,
              w_router, b_router, w_gate, w_up, w_down):
    alpha = (2.0 * DEPTH) ** 0.25
    n_past = page_table.shape[1] * PAGE_SIZE
    n_dec, t_dec = x_sample.shape[:2]
    wb = cache_win.shape[2]
    xp, xs = x_prompt, x_sample
    kv_p, win_p, kv_s, win_s, gv_s = [], [], [], [], []
    for l in range(DEPTH):
        u, v, q, *kv_new, g_nsa, g_a, g_b = split_projection(xp, w_in[l])
        a, _ = gmlp_branch(u, v, gmlp_ln_g[l], gmlp_ln_b[l], gmlp_ws[l], gmlp_bs[l])
        k_cm, v_cm, k_sl, v_sl, k_wn, v_wn = [heads_kv(t) for t in kv_new]
        ctx = nsa_context(k_cm, v_cm, k_sl, v_sl, cmp_pos[l], cmp_w1[l], cmp_w2[l])
        o = nsa_prompt(heads_q(q), nsa_gates(g_nsa), ctx, k_wn, v_wn)
        mix = merge_branches(a, o, g_a, g_b, w_branch_a[l], w_branch_b[l], w_out[l])
        xp = layer_norm(alpha * xp + mix, ln1_g[l], ln1_b[l])
        xp = layer_norm(alpha * xp + moe_ffn(xp, w_router, b_router, w_gate[l], w_up[l], w_down[l]), ln2_g[l], ln2_b[l])
        kv_p.append(jnp.stack([k_cm, v_cm, k_sl, v_sl], axis=2))
        win_p.append(jnp.stack([k_wn, v_wn], axis=2)[:, -min(WINDOW, k_wn.shape[1]):])

        u, v, q, *kv_new, g_nsa, g_a, g_b = split_projection(xs, w_in[l])
        a, vn = gmlp_branch(u, v, gmlp_ln_g[l], gmlp_ln_b[l], gmlp_ws[l], gmlp_bs[l])
        new = [heads_kv(t) for t in kv_new]
        new_kv = jnp.stack(new[:4], axis=2)
        past = cache_kv[page_table, l].reshape(n_dec, n_past, N_KV_SLOTS, KV_HEADS, HEAD_DIM)
        full = jnp.concatenate([past, new_kv.astype(past.dtype)], axis=1)
        ctx = nsa_context(full[:, :, 0], full[:, :, 1], full[:, :, 2], full[:, :, 3], cmp_pos[l], cmp_w1[l], cmp_w2[l])
        win = jnp.concatenate([cache_win[:, l], jnp.stack(new[4:], axis=2).astype(cache_win.dtype)], axis=1)
        qpos = n_past + jnp.arange(t_dec)
        wpos = n_past - wb + jnp.arange(wb + t_dec)
        o = nsa_block(heads_q(q), qpos, nsa_gates(g_nsa), ctx, win[:, :, 0], win[:, :, 1], wpos)
        o = o.reshape(n_dec, t_dec, D_ATTN)
        mix = merge_branches(a, o, g_a, g_b, w_branch_a[l], w_branch_b[l], w_out[l])
        xs = layer_norm(alpha * xs + mix, ln1_g[l], ln1_b[l])
        xs = layer_norm(alpha * xs + moe_ffn(xs, w_router, b_router, w_gate[l], w_up[l], w_down[l]), ln2_g[l], ln2_b[l])
        kv_s.append(new_kv)
        win_s.append(win[:, -wb:])
        gv_s.append(vn)
    kv_rows_prompt = jnp.stack(kv_p, axis=1)
    win_prompt = jnp.stack(win_p, axis=1)
    kv_rows_sample = jnp.stack(kv_s, axis=1)
    win_sample = jnp.stack(win_s, axis=1)
    gmlp_v_sample = jnp.stack(gv_s, axis=1)
    return (xp, xs, kv_rows_prompt, win_prompt, kv_rows_sample, win_sample, gmlp_v_sample)
```

```python
import functools

import numpy as np
import jax
import jax.numpy as jnp
from jax import lax
from jax.experimental import pallas as pl
from jax.experimental.pallas import tpu as pltpu

D_MODEL = 1024
CHUNK = 128
D_GMLP = 1024
GMLP_GROUPS = 8
N_HEADS = 16
KV_HEADS = 2
Q_PER_KV = N_HEADS // KV_HEADS
HEAD_DIM = 64
D_ATTN = N_HEADS * HEAD_DIM
D_KV = KV_HEADS * HEAD_DIM
CMP_BLOCK = 32
CMP_STRIDE = 16
CMP_HIDDEN = 2 * HEAD_DIM
SEL_BLOCK = 64
N_SELECT = 16
WINDOW = 512
Q_BLOCK = 128
FORCE_BONUS = 1e4
N_KV_SLOTS = 4
N_EXPERTS = 16
N_EXPERT_GROUPS = 4
EXPERTS_PER_GROUP = 4
D_EXPERT = 512
PAGE_SIZE = 128
PROJ_SIZES = (D_GMLP, D_GMLP, D_ATTN) + (D_KV,) * 6 + (3 * N_HEADS, D_MODEL, D_MODEL)

LANES = 128
SUBLANES = 8
VMEM_LIMIT_BYTES = 56 * 1024 * 1024

NEG = -1e30
BF = jnp.bfloat16
F32 = jnp.float32
ROUTE_W = LANES
XR_W = D_MODEL + ROUTE_W


def _cparams(sem):
    return pltpu.CompilerParams(dimension_semantics=sem, vmem_limit_bytes=VMEM_LIMIT_BYTES)


def _layer_norm(x, g, b, eps=1e-5):
    mu = jnp.mean(x, -1, keepdims=True)
    xc = x - mu
    var = jnp.mean(xc * xc, -1, keepdims=True)
    return xc * lax.rsqrt(var + eps) * g + b


def _dot(a, b):
    return jnp.dot(a, b, preferred_element_type=F32)


def _dot_nt(a, b):
    return lax.dot_general(a, b, (((1,), (1,)), ((), ())), preferred_element_type=F32)


def _dot_tn(a, b):
    return lax.dot_general(a, b, (((0,), (0,)), ((), ())), preferred_element_type=F32)


def _full(shape):
    nd = len(shape)
    return pl.BlockSpec(shape, lambda *_: (0,) * nd)


def _gmlp_kernel(x_ref, wu_ref, wv_ref, wga_ref, wpa_ref, lng_ref, lnb_ref, ws_ref, bs_ref,
                 ya_ref, *rest, sample):
    xb = x_ref[...].astype(BF)
    u = jax.nn.gelu(_dot(xb, wu_ref[...]))
    v = jax.nn.gelu(_dot(xb, wv_ref[...]))
    vn = _layer_norm(v, lng_ref[...], lnb_ref[...])
    tm = xb.shape[0]
    if sample:
        a = u * (vn * ws_ref[...] + bs_ref[...])
        rest[0][...] = vn
    else:
        n_chunk = tm // CHUNK
        vb = vn.astype(BF)
        row = lax.broadcasted_iota(jnp.int32, (CHUNK, CHUNK), 0)
        col = lax.broadcasted_iota(jnp.int32, (CHUNK, CHUNK), 1)
        causal = col <= row
        bs = bs_ref[...]
        cols = []
        for g in range(GMLP_GROUPS):
            lo = g * CHUNK
            wg = jnp.where(causal, ws_ref[g], 0.0).astype(BF)
            vg = jnp.concatenate([vb[c * CHUNK:(c + 1) * CHUNK, lo:lo + CHUNK] for c in range(n_chunk)], axis=1)
            mixed = _dot(wg, vg) + bs[:, g:g + 1]
            cols.append(jnp.concatenate([mixed[:, c * CHUNK:(c + 1) * CHUNK] for c in range(n_chunk)], axis=0))
        a = u * jnp.concatenate(cols, axis=1)
    ga = jax.nn.sigmoid(_dot(xb, wga_ref[...]))
    ya_ref[...] = ga * _dot(a.astype(BF), wpa_ref[...])


def _gmlp_call(x, wu, wv, wga, wpa, lng, lnb, ws, bs, *, tm, sample):
    n = x.shape[0]
    wspec = _full((D_MODEL, D_MODEL))
    vspec = _full((1, D_MODEL))
    in_specs = [pl.BlockSpec((tm, D_MODEL), lambda i: (i, 0)), wspec, wspec, wspec, wspec, vspec, vspec,
                _full(ws.shape), _full(bs.shape)]
    tok = pl.BlockSpec((tm, D_MODEL), lambda i: (i, 0))
    out_shape = [jax.ShapeDtypeStruct((n, D_MODEL), F32)]
    out_specs = [tok]
    if sample:
        out_shape.append(jax.ShapeDtypeStruct((n, D_MODEL), F32))
        out_specs.append(tok)
    return pl.pallas_call(
        functools.partial(_gmlp_kernel, sample=sample),
        grid=(n // tm,), in_specs=in_specs, out_specs=out_specs, out_shape=out_shape,
        compiler_params=_cparams(("arbitrary",)), name="gmlp_sample" if sample else "gmlp_prompt",
    )(x, wu, wv, wga, wpa, lng, lnb, ws, bs)


def _attn_in_prompt_kernel(x_ref, wgb_ref, wkv_ref, wqt_ref, wkvt_ref, wgnt_ref,
                           sgb_ref, kv_ref, kvb_ref, kvt_ref, qt_ref, gnt_ref):
    xb = x_ref[...].astype(BF)
    sgb_ref[...] = jax.nn.sigmoid(_dot(xb, wgb_ref[...]))
    kv = _dot(xb, wkv_ref[...])
    kv_ref[...] = kv
    kvb_ref[...] = kv.astype(BF)
    kvt_ref[...] = _dot_nt(wkvt_ref[...], xb).astype(BF)
    qt_ref[...] = (_dot_nt(wqt_ref[...], xb) * (HEAD_DIM ** -0.5)).astype(BF)
    gnt_ref[...] = jax.nn.sigmoid(_dot_nt(wgnt_ref[...], xb))


def _attn_in_prompt_call(x, wgb, wkv, wqt, wkvt, wgnt, *, tm):
    n = x.shape[0]
    nkv = wkv.shape[1]
    ngn = wgnt.shape[0]
    in_specs = [pl.BlockSpec((tm, D_MODEL), lambda i: (i, 0)), _full(wgb.shape), _full(wkv.shape),
                _full(wqt.shape), _full(wkvt.shape), _full(wgnt.shape)]
    out_shape = [jax.ShapeDtypeStruct((n, D_MODEL), F32), jax.ShapeDtypeStruct((n, nkv), F32),
                 jax.ShapeDtypeStruct((n, nkv), BF), jax.ShapeDtypeStruct((nkv, n), BF),
                 jax.ShapeDtypeStruct((D_ATTN, n), BF), jax.ShapeDtypeStruct((ngn, n), F32)]
    out_specs = [pl.BlockSpec((tm, D_MODEL), lambda i: (i, 0)), pl.BlockSpec((tm, nkv), lambda i: (i, 0)),
                 pl.BlockSpec((tm, nkv), lambda i: (i, 0)), pl.BlockSpec((nkv, tm), lambda i: (0, i)),
                 pl.BlockSpec((D_ATTN, tm), lambda i: (0, i)), pl.BlockSpec((ngn, tm), lambda i: (0, i))]
    return pl.pallas_call(
        _attn_in_prompt_kernel, grid=(n // tm,), in_specs=in_specs, out_specs=out_specs, out_shape=out_shape,
        compiler_params=_cparams(("arbitrary",)), name="attn_in_prompt",
    )(x, wgb, wkv, wqt, wkvt, wgnt)


def _attn_in_sample_kernel(x_ref, wgb_ref, wkv_ref, wq_ref, wgn_ref, sgb_ref, kv_ref, q_ref, gn_ref):
    xb = x_ref[...].astype(BF)
    sgb_ref[...] = jax.nn.sigmoid(_dot(xb, wgb_ref[...]))
    kv_ref[...] = _dot(xb, wkv_ref[...])
    q_ref[...] = _dot(xb, wq_ref[...]) * (HEAD_DIM ** -0.5)
    gn_ref[...] = jax.nn.sigmoid(_dot(xb, wgn_ref[...]))


def _attn_in_sample_call(x, wgb, wkv, wq, wgn):
    n = x.shape[0]
    nkv = wkv.shape[1]
    out_shape = [jax.ShapeDtypeStruct((n, D_MODEL), F32), jax.ShapeDtypeStruct((n, nkv), F32),
                 jax.ShapeDtypeStruct((n, D_ATTN), F32), jax.ShapeDtypeStruct((n, wgn.shape[1]), F32)]
    return pl.pallas_call(
        _attn_in_sample_kernel, grid=(1,),
        in_specs=[_full(x.shape), _full(wgb.shape), _full(wkv.shape), _full(wq.shape), _full(wgn.shape)],
        out_specs=[_full(s.shape) for s in out_shape], out_shape=out_shape,
        compiler_params=_cparams(("arbitrary",)), name="attn_in_sample",
    )(x, wgb, wkv, wq, wgn)


def _compress_rows(src_refs, n_sub, wp_ref, w2p_ref, w2pt_ref, ppos_ref):
    outs = []
    for slot in range(2):
        src_ref = src_refs[slot]
        x = jnp.concatenate(
            [src_ref[pl.ds(s, n_sub, stride=CMP_STRIDE), :].astype(BF) for s in range(CMP_STRIDE)],
            axis=1)
        hh = _dot(x, wp_ref[slot])
        pp = _dot(ppos_ref[slot], wp_ref[slot])
        pos = pp[0:1, 0:2 * CMP_HIDDEN] + pp[1:2, 2 * CMP_HIDDEN:]
        h1 = pltpu.roll(hh[:, 2 * CMP_HIDDEN:], n_sub - 1, 0)
        g = jax.nn.gelu(hh[:, 0:2 * CMP_HIDDEN] + h1 + pos).astype(BF)
        if slot == 0:
            outs.append(_dot(g, w2p_ref[0]))
        else:
            outs.append(_dot_nt(w2pt_ref[1], g))
    return outs


def _compress_kernel(kcm_ref, vcm_ref, wp_ref, w2p_ref, w2pt_ref, ppos_ref, kc_ref, vct_ref):
    n_sub = kc_ref.shape[0]
    kc, vct = _compress_rows((kcm_ref, vcm_ref), n_sub, wp_ref, w2p_ref, w2pt_ref, ppos_ref)
    kc_ref[...] = kc.astype(BF)
    vct_ref[...] = vct.astype(BF)


def _compress_call(kv, cw, *, batch, seq):
    n_sub = seq // CMP_STRIDE
    wp, w2p, w2pt, ppos = cw
    return pl.pallas_call(
        _compress_kernel, grid=(batch,),
        in_specs=[pl.BlockSpec((seq, LANES), lambda b: (b, 0)), pl.BlockSpec((seq, LANES), lambda b: (b, 1)),
                  _full(wp.shape), _full(w2p.shape), _full(w2pt.shape), _full(ppos.shape)],
        out_specs=[pl.BlockSpec((None, n_sub, LANES), lambda b: (b, 0, 0)),
                   pl.BlockSpec((None, LANES, n_sub), lambda b: (b, 0, 0))],
        out_shape=[jax.ShapeDtypeStruct((batch, n_sub, LANES), BF), jax.ShapeDtypeStruct((batch, LANES, n_sub), BF)],
        compiler_params=_cparams(("arbitrary",)), name="compress_prompt",
    )(kv, kv, wp, w2p, w2pt, ppos)


def _top_blocks(score):
    n_blk = score.shape[0]
    jf = lax.broadcasted_iota(jnp.int32, score.shape, 0).astype(F32)
    work = score
    sel = jnp.zeros_like(score)
    for _ in range(N_SELECT):
        mx = jnp.max(work, axis=0, keepdims=True)
        first = jnp.min(jnp.where(work == mx, jf, float(n_blk)), axis=0, keepdims=True)
        pick = jf == first
        sel = jnp.where(pick, 1.0, sel)
        work = jnp.where(pick, -3e38, work)
    return sel


def _nsa_prompt_kernel(qt_ref, gn_ref, ksel_ref, kwin_ref, vselt_ref, vwint_ref, kc_ref, vct_ref, ovt_ref,
                       o_ref, sel_ref, m_ref, l_ref, acc_ref):
    i = pl.program_id(1)
    q0 = i * Q_BLOCK
    n_cmp = kc_ref.shape[0]
    n_selb = ovt_ref.shape[0]
    nq = Q_PER_KV * Q_BLOCK
    qpos = q0 + lax.broadcasted_iota(jnp.int32, (1, Q_BLOCK), 1)
    cmp_end = lax.broadcasted_iota(jnp.int32, (n_cmp, Q_BLOCK), 0) * CMP_STRIDE + (CMP_BLOCK - 1)
    vis = cmp_end <= qpos
    blk = lax.broadcasted_iota(jnp.int32, (n_selb, Q_BLOCK), 0)
    cur = qpos // SEL_BLOCK
    forced = (blk == 0) | (blk == cur) | (blk == cur - 1)
    valid = blk * SEL_BLOCK <= qpos
    krow = lax.broadcasted_iota(jnp.int32, (Q_BLOCK, Q_BLOCK), 0)
    tcol = lax.broadcasted_iota(jnp.int32, (Q_BLOCK, Q_BLOCK), 1)
    zeros_q = jnp.zeros((HEAD_DIM, nq), BF)

    def lanes8(x):
        return jnp.concatenate([x] * Q_PER_KV, axis=1)

    def flash_step(k_tile, vt_tile, qtp, mask):
        s = jnp.where(lanes8(mask), _dot(k_tile, qtp), NEG)
        m_old = m_ref[...]
        m_new = jnp.maximum(m_old, jnp.max(s, axis=0, keepdims=True))
        alpha = jnp.exp(m_old - m_new)
        p = jnp.exp(s - m_new)
        l_ref[...] = alpha * l_ref[...] + jnp.sum(p, axis=0, keepdims=True)
        acc_ref[...] = alpha * acc_ref[...] + _dot(vt_tile, p.astype(BF))
        m_ref[...] = m_new

    def flash_init():
        m_ref[...] = jnp.full(m_ref.shape, NEG, F32)
        l_ref[...] = jnp.zeros(l_ref.shape, F32)
        acc_ref[...] = jnp.zeros(acc_ref.shape, F32)

    for h in range(KV_HEADS):
        qh = qt_ref[h * Q_PER_KV * HEAD_DIM:(h + 1) * Q_PER_KV * HEAD_DIM, :]
        qcat = jnp.concatenate([qh[g * HEAD_DIM:(g + 1) * HEAD_DIM, :] for g in range(Q_PER_KV)], axis=1)
        qtp = jnp.concatenate([qcat, zeros_q] if h == 0 else [zeros_q, qcat], axis=0)
        gates = [jnp.concatenate([gn_ref[h * 3 * Q_PER_KV + c * Q_PER_KV + g:h * 3 * Q_PER_KV + c * Q_PER_KV + g + 1, :]
                                  for g in range(Q_PER_KV)], axis=1) for c in range(3)]

        sc = _dot(kc_ref[...], qtp)
        psum = jnp.zeros((n_cmp, Q_BLOCK), F32)
        pcols = []
        for g in range(Q_PER_KV):
            s = jnp.where(vis, sc[:, g * Q_BLOCK:(g + 1) * Q_BLOCK], NEG)
            mx = jnp.max(s, axis=0, keepdims=True)
            p = jnp.where(vis, jnp.exp(s - mx), 0.0)
            den = jnp.maximum(jnp.sum(p, axis=0, keepdims=True), 1e-30)
            p = p * (1.0 / den)
            psum = psum + p
            pcols.append(p.astype(BF))
        o_c = _dot(vct_ref[h * HEAD_DIM:(h + 1) * HEAD_DIM, :], jnp.concatenate(pcols, axis=1))

        p_hi = psum.astype(BF)
        p_lo = (psum - p_hi.astype(F32)).astype(BF)
        imp = _dot(ovt_ref[...], p_hi) + _dot(ovt_ref[...], p_lo)
        score = jnp.where(valid, imp + jnp.where(forced, FORCE_BONUS, 0.0), NEG)
        sel_ref[...] = _top_blocks(score)

        flash_init()

        def sel_body(kt, carry):
            k0 = pl.multiple_of(kt * Q_BLOCK, Q_BLOCK)
            r0 = sel_ref[pl.ds(2 * kt, 1), :]
            r1 = sel_ref[pl.ds(2 * kt + 1, 1), :]
            picked = jnp.concatenate([jnp.broadcast_to(r0, (SEL_BLOCK, Q_BLOCK)),
                                      jnp.broadcast_to(r1, (SEL_BLOCK, Q_BLOCK))], axis=0) > 0.5
            mask = picked & (k0 + krow <= q0 + tcol)
            flash_step(ksel_ref[pl.ds(k0, Q_BLOCK), :], vselt_ref[h * HEAD_DIM:(h + 1) * HEAD_DIM, pl.ds(k0, Q_BLOCK)],
                       qtp, mask)
            return carry

        lax.fori_loop(0, i + 1, sel_body, 0)
        o_s = acc_ref[...] * (1.0 / l_ref[...])

        flash_init()

        def win_body(kt, carry):
            k0 = pl.multiple_of(kt * Q_BLOCK, Q_BLOCK)
            rel = (q0 + tcol) - (k0 + krow)
            mask = (rel >= 0) & (rel < WINDOW)
            flash_step(kwin_ref[pl.ds(k0, Q_BLOCK), :], vwint_ref[h * HEAD_DIM:(h + 1) * HEAD_DIM, pl.ds(k0, Q_BLOCK)],
                       qtp, mask)
            return carry

        lax.fori_loop(jnp.maximum(i - WINDOW // Q_BLOCK, 0), i + 1, win_body, 0)
        o_w = acc_ref[...] * (1.0 / l_ref[...])

        tot = gates[0] * o_c + gates[1] * o_s + gates[2] * o_w
        for g in range(Q_PER_KV):
            r = (h * Q_PER_KV + g) * HEAD_DIM
            o_ref[r:r + HEAD_DIM, :] = tot[:, g * Q_BLOCK:(g + 1) * Q_BLOCK].astype(BF)


def _nsa_prompt_call(qt, gnt, kvb, kvt, kc, vct, ovt, *, batch, seq):
    nblk = seq // Q_BLOCK
    n = batch * seq
    n_sub = seq // CMP_STRIDE
    n_selb = seq // SEL_BLOCK
    nq = Q_PER_KV * Q_BLOCK
    in_specs = [
        pl.BlockSpec((D_ATTN, Q_BLOCK), lambda b, i: (0, b * nblk + i)),
        pl.BlockSpec((gnt.shape[0], Q_BLOCK), lambda b, i: (0, b * nblk + i)),
        pl.BlockSpec((seq, LANES), lambda b, i: (b, 2)),
        pl.BlockSpec((seq, LANES), lambda b, i: (b, 4)),
        pl.BlockSpec((LANES, seq), lambda b, i: (3, b)),
        pl.BlockSpec((LANES, seq), lambda b, i: (5, b)),
        pl.BlockSpec((None, n_sub, LANES), lambda b, i: (b, 0, 0)),
        pl.BlockSpec((None, LANES, n_sub), lambda b, i: (b, 0, 0)),
        _full(ovt.shape),
    ]
    return pl.pallas_call(
        _nsa_prompt_kernel, grid=(batch, nblk), in_specs=in_specs,
        out_specs=pl.BlockSpec((D_ATTN, Q_BLOCK), lambda b, i: (0, b * nblk + i)),
        out_shape=jax.ShapeDtypeStruct((D_ATTN, n), BF),
        scratch_shapes=[pltpu.VMEM((n_selb, Q_BLOCK), F32), pltpu.VMEM((1, nq), F32), pltpu.VMEM((1, nq), F32),
                        pltpu.VMEM((HEAD_DIM, nq), F32)],
        compiler_params=_cparams(("arbitrary", "arbitrary")), name="nsa_prompt",
    )(qt, gnt, kvb, kvb, kvt, kvt, kc, vct, ovt)


def _route(x1, wrt_ref, br_ref, tri_ref, cnt_ref):
    tm = x1.shape[0]
    logits = lax.dot_general(wrt_ref[...], x1, (((1,), (1,)), ((), ())), precision=lax.Precision.HIGHEST,
                             preferred_element_type=F32)
    aff = jax.nn.sigmoid(logits)
    grp = aff + br_ref[...]
    affr = [aff[k:k + 1, :] for k in range(N_EXPERTS)]
    grpr = [grp[k:k + 1, :] for k in range(N_EXPERTS)]
    best = None
    gsel = jnp.zeros((1, tm), jnp.int32)
    for gi in range(N_EXPERT_GROUPS):
        m = grpr[gi * 4:(gi + 1) * 4]
        top2 = None
        for a in range(4):
            for b in range(a + 1, 4):
                s = m[a] + m[b]
                top2 = s if top2 is None else jnp.maximum(top2, s)
        if best is None:
            best = top2
        else:
            better = top2 > best
            gsel = jnp.where(better, gi, gsel)
            best = jnp.where(better, top2, best)
    ing, ina = [], []
    for j in range(4):
        vg, va = grpr[j], affr[j]
        for gi in range(1, N_EXPERT_GROUPS):
            vg = jnp.where(gsel == gi, grpr[gi * 4 + j], vg)
            va = jnp.where(gsel == gi, affr[gi * 4 + j], va)
        ing.append(vg)
        ina.append(va)

    def argmax4(vals):
        bv, bi = vals[0], jnp.zeros((1, tm), jnp.int32)
        for j in range(1, 4):
            better = vals[j] > bv
            bi = jnp.where(better, j, bi)
            bv = jnp.where(better, vals[j], bv)
        return bi

    loc1 = argmax4(ing)
    loc2 = argmax4([jnp.where(loc1 == j, -3e38, ing[j]) for j in range(4)])
    w1 = sum(jnp.where(loc1 == j, ina[j], 0.0) for j in range(4))
    w2 = sum(jnp.where(loc2 == j, ina[j], 0.0) for j in range(4))
    tot = w1 + w2
    wd = [jnp.where(loc1 == j, w1 / tot, 0.0) + jnp.where(loc2 == j, w2 / tot, 0.0) for j in range(4)]
    oh = jnp.concatenate([(gsel == gi).astype(F32) for gi in range(N_EXPERT_GROUPS)]
                         + [jnp.zeros((SUBLANES - N_EXPERT_GROUPS, tm), F32)], axis=0)
    cum = _dot(oh.astype(BF), tri_ref[...])
    carry = cnt_ref[...][:, 0:1]
    rank = jnp.sum(oh * (cum + carry), axis=0, keepdims=True)
    cnt_ref[...] = cnt_ref[...] + jnp.sum(oh, axis=1, keepdims=True)
    rows = wd + [gsel.astype(F32), rank, jnp.zeros((ROUTE_W - 6, tm), F32)]
    return jnp.concatenate(rows, axis=0)


def _merge_kernel(x_ref, ya_ref, sgb_ref, o_ref, wpb_ref, wo_ref, g_ref, b_ref, wrt_ref, br_ref, tri_ref,
                  xr_ref, cnt_out_ref, cnt_ref, *, alpha, o_transposed):
    @pl.when(pl.program_id(0) == 0)
    def _():
        cnt_ref[...] = jnp.zeros(cnt_ref.shape, F32)

    if o_transposed:
        ob = _dot_tn(o_ref[...], wpb_ref[...])
    else:
        ob = _dot(o_ref[...].astype(BF), wpb_ref[...])
    y = ya_ref[...] + sgb_ref[...] * ob
    mix = _dot(y.astype(BF), wo_ref[...])
    x1 = _layer_norm(alpha * x_ref[...] + mix, g_ref[...], b_ref[...])
    xr_ref[:, 0:D_MODEL] = x1
    info = _route(x1, wrt_ref, br_ref, tri_ref, cnt_ref)
    xr_ref[:, D_MODEL:XR_W] = info.T
    cnt_out_ref[...] = cnt_ref[...]


def _merge_call(x, ya, sgb, o, wpb, wo, g, b, wrt, br, tri, *, tm, alpha, o_transposed):
    n = ya.shape[0]
    tok = pl.BlockSpec((tm, D_MODEL), lambda i: (i, 0))
    ospec = pl.BlockSpec((D_ATTN, tm), lambda i: (0, i)) if o_transposed else tok
    return pl.pallas_call(
        functools.partial(_merge_kernel, alpha=alpha, o_transposed=o_transposed), grid=(n // tm,),
        in_specs=[tok, tok, tok, ospec, _full(wpb.shape), _full(wo.shape), _full(g.shape), _full(b.shape),
                  _full(wrt.shape), _full(br.shape), _full(tri.shape)],
        out_specs=[pl.BlockSpec((tm, XR_W), lambda i: (i, 0)), _full((SUBLANES, LANES))],
        out_shape=[jax.ShapeDtypeStruct((n, XR_W), F32), jax.ShapeDtypeStruct((SUBLANES, LANES), F32)],
        scratch_shapes=[pltpu.VMEM((SUBLANES, LANES), F32)],
        compiler_params=_cparams(("arbitrary",)), name="merge_t" if o_transposed else "merge_n",
    )(x, ya, sgb, o, wpb, wo, g, b, wrt, br, tri)


def _row_copy(src_ref, dst_ref, sem, src_row, dst_row):
    return pltpu.make_async_copy(src_ref.at[pl.ds(src_row, 1), :], dst_ref.at[pl.ds(dst_row, 1), :], sem)


def _scatter_kernel(dest_ref, xr_ref, xs_in_ref, xs_ref, sem):
    del xs_in_ref
    tm = xr_ref.shape[0]

    def start(r, c):
        _row_copy(xr_ref, xs_ref, sem, r, dest_ref[r]).start()
        return c

    def wait(r, c):
        _row_copy(xr_ref, xs_ref, sem, r, dest_ref[r]).wait()
        return c

    lax.fori_loop(0, tm, start, 0)
    lax.fori_loop(0, tm, wait, 0)


def _scatter_call(dest, xr, xs0, *, tm):
    n = xr.shape[0]
    return pl.pallas_call(
        _scatter_kernel, grid=(n // tm,),
        in_specs=[pl.BlockSpec((tm,), lambda i: (i,), memory_space=pltpu.SMEM),
                  pl.BlockSpec((tm, XR_W), lambda i: (i, 0)), pl.BlockSpec(memory_space=pl.ANY)],
        out_specs=pl.BlockSpec(memory_space=pl.ANY),
        out_shape=jax.ShapeDtypeStruct(xs0.shape, F32),
        scratch_shapes=[pltpu.SemaphoreType.DMA(())],
        input_output_aliases={2: 0},
        compiler_params=_cparams(("arbitrary",)), name="moe_scatter",
    )(dest, xr, xs0)


def _experts_kernel(tg_ref, nu_ref, xs_ref, wg_ref, wu_ref, wd_ref, ys_ref):
    del tg_ref
    j = pl.program_id(0)

    @pl.when(j < nu_ref[0])
    def _():
        xb = xs_ref[:, 0:D_MODEL].astype(BF)
        acc = None
        for e in range(EXPERTS_PER_GROUP):
            hidden = jax.nn.silu(_dot(xb, wg_ref[e])) * _dot(xb, wu_ref[e])
            hidden = hidden * xs_ref[:, D_MODEL + e:D_MODEL + e + 1]
            y = _dot(hidden.astype(BF), wd_ref[e])
            acc = y if acc is None else acc + y
        ys_ref[...] = acc

    @pl.when(j >= nu_ref[0])
    def _():
        ys_ref[...] = jnp.zeros(ys_ref.shape, F32)


def _experts_call(tile_group, n_used, xs, wg, wu, wd, *, tmx):
    rows = xs.shape[0]
    wspec_in = pl.BlockSpec((None, EXPERTS_PER_GROUP, D_MODEL, D_EXPERT), lambda j, tg, nu: (tg[j], 0, 0, 0))
    wspec_out = pl.BlockSpec((None, EXPERTS_PER_GROUP, D_EXPERT, D_MODEL), lambda j, tg, nu: (tg[j], 0, 0, 0))
    grid_spec = pltpu.PrefetchScalarGridSpec(
        num_scalar_prefetch=2, grid=(rows // tmx,),
        in_specs=[pl.BlockSpec((tmx, XR_W), lambda j, tg, nu: (j, 0)), wspec_in, wspec_in, wspec_out],
        out_specs=pl.BlockSpec((tmx, D_MODEL), lambda j, tg, nu: (j, 0)))
    return pl.pallas_call(
        _experts_kernel, grid_spec=grid_spec, out_shape=jax.ShapeDtypeStruct((rows, D_MODEL), F32),
        compiler_params=_cparams(("arbitrary",)), name="moe_experts",
    )(tile_group, n_used, xs, wg, wu, wd)


def _combine_kernel(dest_ref, xr_ref, ys_ref, g_ref, b_ref, x2_ref, buf_ref, sem, *, alpha):
    tm = xr_ref.shape[0]

    def start(r, c):
        _row_copy(ys_ref, buf_ref, sem, dest_ref[r], r).start()
        return c

    def wait(r, c):
        _row_copy(ys_ref, buf_ref, sem, dest_ref[r], r).wait()
        return c

    lax.fori_loop(0, tm, start, 0)
    lax.fori_loop(0, tm, wait, 0)
    x2_ref[...] = _layer_norm(alpha * xr_ref[...] + buf_ref[...], g_ref[...], b_ref[...])


def _combine_call(dest, xr, ys, g, b, *, tm, alpha):
    n = xr.shape[0]
    return pl.pallas_call(
        functools.partial(_combine_kernel, alpha=alpha), grid=(n // tm,),
        in_specs=[pl.BlockSpec((tm,), lambda i: (i,), memory_space=pltpu.SMEM),
                  pl.BlockSpec((tm, D_MODEL), lambda i: (i, 0)), pl.BlockSpec(memory_space=pl.ANY),
                  _full(g.shape), _full(b.shape)],
        out_specs=pl.BlockSpec((tm, D_MODEL), lambda i: (i, 0)),
        out_shape=jax.ShapeDtypeStruct((n, D_MODEL), F32),
        scratch_shapes=[pltpu.VMEM((tm, D_MODEL), F32), pltpu.SemaphoreType.DMA(())],
        compiler_params=_cparams(("arbitrary",)), name="moe_combine",
    )(dest, xr, ys, g, b)


def _moe(xr, cnt, lw, *, tm, tmx, alpha):
    n = xr.shape[0]
    n_tiles = n // tmx + N_EXPERT_GROUPS
    counts = cnt[:N_EXPERT_GROUPS, 0].astype(jnp.int32)
    tiles_per = (counts + tmx - 1) // tmx
    tile_end = jnp.cumsum(tiles_per)
    offs = (tile_end - tiles_per) * tmx
    gid = xr[:, D_MODEL + 4].astype(jnp.int32)
    rank = xr[:, D_MODEL + 5].astype(jnp.int32)
    dest = offs[gid] + rank
    tile_group = jnp.minimum(jnp.sum(jnp.arange(n_tiles)[:, None] >= tile_end[None, :], axis=1),
                             N_EXPERT_GROUPS - 1).astype(jnp.int32)
    n_used = tile_end[-1:].astype(jnp.int32)
    xs = _scatter_call(dest, xr, jnp.zeros((n_tiles * tmx, XR_W), F32), tm=tm)
    ys = _experts_call(tile_group, n_used, xs, lw["wg"], lw["wu_e"], lw["wd"], tmx=tmx)
    return _combine_call(dest, xr, ys, lw["ln2_g"], lw["ln2_b"], tm=tm, alpha=alpha)


def _softmax_with_new(s, mask, s_new, new_on):
    s = jnp.where(mask, s, NEG)
    s_new = jnp.where(new_on, s_new, NEG)
    mx = jnp.maximum(jnp.max(s, axis=1, keepdims=True), s_new)
    p = jnp.where(mask, jnp.exp(s - mx), 0.0)
    p_new = jnp.where(new_on, jnp.exp(s_new - mx), 0.0)
    den = jnp.maximum(jnp.sum(p, axis=1, keepdims=True) + p_new, 1e-30)
    return p, p_new, 1.0 / den


def _nsa_sample_kernel(pt_ref, q_ref, new_ref, gate_ref, cw_ref, cache_ref, wp_ref, w2p_ref, w2pt_ref, ppos_ref,
                       ov_ref, exp_ref, o_ref, buf_ref, sem, *, layer, n_pages, n_past):
    b = pl.program_id(0)
    n_sub = n_past // CMP_STRIDE
    n_cmp = n_sub - CMP_BLOCK // CMP_STRIDE + 1
    nrow = KV_HEADS * Q_PER_KV

    def page_copy(p, slot):
        return pltpu.make_async_copy(
            cache_ref.at[pt_ref[b * n_pages + p], layer, :, pl.ds(slot * LANES, LANES)],
            buf_ref.at[slot, pl.ds(p * PAGE_SIZE, PAGE_SIZE), :], sem)

    for p in range(n_pages):
        for slot in range(N_KV_SLOTS):
            page_copy(p, slot).start()
    for p in range(n_pages):
        for slot in range(N_KV_SLOTS):
            page_copy(p, slot).wait()

    q = q_ref[...]
    qf = q.astype(F32)
    new = new_ref[...]
    qpos = n_past

    def new_score(col):
        kn = new[:, col * LANES:(col + 1) * LANES].astype(BF).astype(F32)
        return jnp.sum(qf * kn, axis=1, keepdims=True)

    def new_value(col):
        return new[:, col * LANES:(col + 1) * LANES].astype(BF).astype(F32)

    kc, vct = _compress_rows((buf_ref.at[0], buf_ref.at[1]), n_sub, wp_ref, w2p_ref, w2pt_ref, ppos_ref)
    s_c = _dot_nt(q, kc.astype(BF))
    n_idx = lax.broadcasted_iota(jnp.int32, (nrow, n_sub), 1)
    vis = (n_idx * CMP_STRIDE + (CMP_BLOCK - 1) <= qpos) & (n_idx < n_cmp)
    s_c = jnp.where(vis, s_c, NEG)
    mx = jnp.max(s_c, axis=1, keepdims=True)
    p_c = jnp.where(vis, jnp.exp(s_c - mx), 0.0)
    p_c = p_c * (1.0 / jnp.maximum(jnp.sum(p_c, axis=1, keepdims=True), 1e-30))
    o_c = _dot_nt(p_c.astype(BF), vct.astype(BF))

    psum = jnp.concatenate([jnp.sum(p_c[h * Q_PER_KV:(h + 1) * Q_PER_KV], axis=0, keepdims=True)
                            for h in range(KV_HEADS)] + [jnp.zeros((SUBLANES - KV_HEADS, n_sub), F32)], axis=0)
    p_hi = psum.astype(BF)
    p_lo = (psum - p_hi.astype(F32)).astype(BF)
    imp = _dot(p_hi, ov_ref[...]) + _dot(p_lo, ov_ref[...])
    n_selp = imp.shape[1]
    j_idx = lax.broadcasted_iota(jnp.int32, (SUBLANES, n_selp), 1)
    cur = qpos // SEL_BLOCK
    forced = (j_idx == 0) | (j_idx == cur) | (j_idx == cur - 1)
    valid = j_idx * SEL_BLOCK <= qpos
    score = jnp.where(valid, imp + jnp.where(forced, FORCE_BONUS, 0.0), NEG)
    score_t = score.T
    jr = lax.broadcasted_iota(jnp.int32, (n_selp, n_selp), 0)
    jc = lax.broadcasted_iota(jnp.int32, (n_selp, n_selp), 1)
    sel_rows = []
    for h in range(KV_HEADS):
        other = jnp.broadcast_to(score_t[:, h:h + 1], (n_selp, n_selp))
        mine = jnp.broadcast_to(score[h:h + 1, :], (n_selp, n_selp))
        ahead = (other > mine) | ((other == mine) & (jr < jc))
        rank = jnp.sum(ahead.astype(F32), axis=0, keepdims=True)
        sel_rows.append(jnp.broadcast_to((rank < N_SELECT).astype(F32), (Q_PER_KV, n_selp)))
    sel = jnp.concatenate(sel_rows, axis=0)

    on = _dot(sel.astype(BF), exp_ref[...]) > 0.5
    ks = buf_ref[2].astype(BF)
    vs = buf_ref[3].astype(BF)
    new_on = sel[:, cur:cur + 1] > 0.5
    p_s, pn_s, inv_s = _softmax_with_new(_dot_nt(q, ks), on, new_score(2), new_on)
    o_s = (_dot(p_s.astype(BF), vs) + pn_s * new_value(3)) * inv_s

    wb = cw_ref.shape[0]
    kw = cw_ref[:, 0:LANES].astype(BF)
    vw = cw_ref[:, LANES:2 * LANES].astype(BF)
    wpos = (n_past - wb) + lax.broadcasted_iota(jnp.int32, (nrow, wb), 1)
    rel = qpos - wpos
    wmask = (rel >= 0) & (rel < WINDOW) & (wpos >= 0)
    p_w, pn_w, inv_w = _softmax_with_new(_dot_nt(q, kw), wmask, new_score(4), jnp.full((nrow, 1), True))
    o_w = (_dot(p_w.astype(BF), vw) + pn_w * new_value(5)) * inv_w

    gates = gate_ref[...]
    o_ref[...] = gates[:, 0:1] * o_c + gates[:, 1:2] * o_s + gates[:, 2:3] * o_w


def _nsa_sample_call(pt, qp, new, gates, cache_win, cache, cw, ov, expand, *, layer, n_past):
    n_dec = qp.shape[0]
    n_pages = n_past // PAGE_SIZE
    nrow = KV_HEADS * Q_PER_KV
    wb = cache_win.shape[2]
    wp, w2p, w2pt, ppos = cw
    const = lambda shape: pl.BlockSpec(shape, lambda b, pt_: (0,) * len(shape))
    grid_spec = pltpu.PrefetchScalarGridSpec(
        num_scalar_prefetch=1, grid=(n_dec,),
        in_specs=[pl.BlockSpec((None, nrow, LANES), lambda b, pt_: (b, 0, 0)),
                  pl.BlockSpec((None, 1, new.shape[2]), lambda b, pt_: (b, 0, 0)),
                  pl.BlockSpec((None, nrow, LANES), lambda b, pt_: (b, 0, 0)),
                  pl.BlockSpec((None, None, wb, 2 * LANES), lambda b, pt_: (b, layer, 0, 0)),
                  pl.BlockSpec(memory_space=pl.ANY),
                  const(wp.shape), const(w2p.shape), const(w2pt.shape), const(ppos.shape), const(ov.shape),
                  const(expand.shape)],
        out_specs=pl.BlockSpec((None, nrow, LANES), lambda b, pt_: (b, 0, 0)),
        scratch_shapes=[pltpu.VMEM((N_KV_SLOTS, n_past, LANES), F32), pltpu.SemaphoreType.DMA(())])
    return pl.pallas_call(
        functools.partial(_nsa_sample_kernel, layer=layer, n_pages=n_pages, n_past=n_past),
        grid_spec=grid_spec, out_shape=jax.ShapeDtypeStruct((n_dec, nrow, LANES), F32),
        compiler_params=_cparams(("arbitrary",)), name="nsa_sample",
    )(pt, qp, new, gates, cache_win, cache, wp, w2p, w2pt, ppos, ov, expand)


def _overlap(n_cmp, n_sel):
    c0 = np.arange(n_cmp) * CMP_STRIDE
    s0 = np.arange(n_sel) * SEL_BLOCK
    return ((c0[:, None] < s0[None, :] + SEL_BLOCK) & (c0[:, None] + CMP_BLOCK > s0[None, :])).astype(np.float32)


def _layer_weights(l, w_in, gmlp_ln_g, gmlp_ln_b, gmlp_ws, gmlp_bs, cmp_pos, cmp_w1, cmp_w2, w_branch_a, w_branch_b,
                   w_out, ln1_g, ln1_b, ln2_g, ln2_b, w_gate, w_up, w_down):
    cuts = np.cumsum((0,) + PROJ_SIZES)
    seg = lambda i, j=None: w_in[l][:, cuts[i]:cuts[i + 1 if j is None else j]].astype(BF)
    wgn = seg(9)
    eye = jnp.eye(KV_HEADS, dtype=F32)
    r = CMP_BLOCK // CMP_STRIDE
    w1r = cmp_w1[l].reshape(2, r, CMP_STRIDE, HEAD_DIM, CMP_HIDDEN)
    wp = jnp.einsum("xrsdf,hk->xshdrkf", w1r, eye).reshape(2, CMP_STRIDE * D_KV, r * KV_HEADS * CMP_HIDDEN)
    w2p = jnp.einsum("xfd,hk->xhfkd", cmp_w2[l], eye).reshape(2, KV_HEADS * CMP_HIDDEN, D_KV)
    pos = cmp_pos[l].reshape(2, r, CMP_STRIDE, 1, HEAD_DIM)
    ppos = jnp.broadcast_to(pos, (2, r, CMP_STRIDE, KV_HEADS, HEAD_DIM)).reshape(2, r, CMP_STRIDE * D_KV)
    ppos = jnp.pad(ppos, ((0, 0), (0, SUBLANES - r), (0, 0)))
    ex = lambda w: w.reshape((N_EXPERT_GROUPS, EXPERTS_PER_GROUP) + w.shape[1:]).astype(BF)
    row = lambda v: v[l][None, :].astype(F32)
    return {
        "wu": seg(0), "wv": seg(1), "wq": seg(2), "wkv": seg(3, 9), "wga": seg(10), "wgb": seg(11),
        "wqt": seg(2).T, "wkvt": seg(3, 9).T,
        "wgnt": wgn.reshape(D_MODEL, KV_HEADS, Q_PER_KV, 3).transpose(1, 3, 2, 0).reshape(3 * N_HEADS, D_MODEL),
        "wgn": jnp.pad(wgn, ((0, 0), (0, LANES - 3 * N_HEADS))),
        "gln_g": row(gmlp_ln_g), "gln_b": row(gmlp_ln_b),
        "ws": gmlp_ws[l], "bs_t": gmlp_bs[l].T,
        "ws0": jnp.repeat(gmlp_ws[l][:, 0, 0], CHUNK)[None, :], "bs0": jnp.repeat(gmlp_bs[l][:, 0], CHUNK)[None, :],
        "cw": (wp.astype(BF), w2p.astype(BF), w2p.transpose(0, 2, 1).astype(BF), ppos.astype(BF)),
        "wpa": w_branch_a[l].astype(BF), "wpb": w_branch_b[l].astype(BF), "wo": w_out[l].astype(BF),
        "ln1_g": row(ln1_g), "ln1_b": row(ln1_b), "ln2_g": row(ln2_g), "ln2_b": row(ln2_b),
        "wg": ex(w_gate[l]), "wu_e": ex(w_up[l]), "wd": ex(w_down[l]),
    }


def _strict_upper(n):
    return jnp.asarray(np.triu(np.ones((n, n), np.float32), 1), BF)


def kernel(x_prompt, x_sample, cache_kv, cache_win, page_table, w_in, gmlp_ln_g, gmlp_ln_b, gmlp_ws, gmlp_bs,
           cmp_pos, cmp_w1, cmp_w2, w_branch_a, w_branch_b, w_out, ln1_g, ln1_b, ln2_g, ln2_b,
           w_router, b_router, w_gate, w_up, w_down):
    depth = w_in.shape[0]
    alpha = (2.0 * depth) ** 0.25
    batch, seq, _ = x_prompt.shape
    n_dec, t_dec = x_sample.shape[:2]
    n_past = page_table.shape[1] * PAGE_SIZE
    assert t_dec == 1 and seq % (4 * CHUNK) == 0 and n_past % SEL_BLOCK == 0 and n_dec % SUBLANES == 0
    tm_p, tmx_p = 4 * CHUNK, 2 * CHUNK
    tm_s = tmx_s = n_dec
    wb = cache_win.shape[2]

    n_sub_p = seq // CMP_STRIDE
    ovt_p = jnp.asarray(np.pad(_overlap(n_sub_p - 1, seq // SEL_BLOCK), ((0, 1), (0, 0))).T, BF)
    n_sub_s = n_past // CMP_STRIDE
    n_sel_s = n_past // SEL_BLOCK + 1
    ov_s = jnp.asarray(np.pad(_overlap(n_sub_s - 1, n_sel_s), ((0, 1), (0, LANES - n_sel_s))), BF)
    expand = jnp.asarray(np.arange(LANES)[:, None] == (np.arange(n_past) // SEL_BLOCK)[None, :], BF)
    wrt = w_router.T.astype(F32)
    br = b_router[:, None].astype(F32)
    tri_p, tri_s = _strict_upper(tm_p), _strict_upper(tm_s)
    cache = cache_kv.reshape(cache_kv.shape[0], depth, PAGE_SIZE, N_KV_SLOTS * D_KV)
    cwin = cache_win.reshape(n_dec, depth, wb, 2 * D_KV)
    pt = page_table.reshape(-1).astype(jnp.int32)

    xp = x_prompt.reshape(batch * seq, D_MODEL)
    xs = x_sample.reshape(n_dec, D_MODEL)
    kv_p, win_p, kv_s, win_s, gv_s = [], [], [], [], []
    for l in range(depth):
        lw = _layer_weights(l, w_in, gmlp_ln_g, gmlp_ln_b, gmlp_ws, gmlp_bs, cmp_pos, cmp_w1, cmp_w2, w_branch_a,
                            w_branch_b, w_out, ln1_g, ln1_b, ln2_g, ln2_b, w_gate, w_up, w_down)
        (ya,) = _gmlp_call(xp, lw["wu"], lw["wv"], lw["wga"], lw["wpa"], lw["gln_g"], lw["gln_b"], lw["ws"],
                           lw["bs_t"], tm=tm_p, sample=False)
        sgb, kv, kvb, kvt, qt, gnt = _attn_in_prompt_call(xp, lw["wgb"], lw["wkv"], lw["wqt"], lw["wkvt"], lw["wgnt"],
                                                          tm=tm_p)
        kc, vct = _compress_call(kv, lw["cw"], batch=batch, seq=seq)
        ot = _nsa_prompt_call(qt, gnt, kvb, kvt, kc, vct, ovt_p, batch=batch, seq=seq)
        xr, cnt = _merge_call(xp, ya, sgb, ot, lw["wpb"], lw["wo"], lw["ln1_g"], lw["ln1_b"], wrt, br, tri_p,
                              tm=tm_p, alpha=alpha, o_transposed=True)
        xp = _moe(xr, cnt, lw, tm=tm_p, tmx=tmx_p, alpha=alpha)
        kvr = kv.reshape(batch, seq, 6, KV_HEADS, HEAD_DIM)
        kv_p.append(kvr[:, :, :N_KV_SLOTS])
        win_p.append(kvr[:, -min(WINDOW, seq):, N_KV_SLOTS:])
        ya, vn = _gmlp_call(xs, lw["wu"], lw["wv"], lw["wga"], lw["wpa"], lw["gln_g"], lw["gln_b"], lw["ws0"],
                            lw["bs0"], tm=tm_s, sample=True)
        sgb, kv, q, gn = _attn_in_sample_call(xs, lw["wgb"], lw["wkv"], lw["wq"], lw["wgn"])
        qh = q.reshape(n_dec, KV_HEADS, Q_PER_KV, HEAD_DIM)
        zq = jnp.zeros_like(qh[:, 0])
        qp = jnp.concatenate([jnp.concatenate([qh[:, 0], zq], -1), jnp.concatenate([zq, qh[:, 1]], -1)], 1).astype(BF)
        gates = jnp.pad(gn[:, :3 * N_HEADS].reshape(n_dec, N_HEADS, 3), ((0, 0), (0, 0), (0, LANES - 3)))
        o16 = _nsa_sample_call(pt, qp, kv[:, None, :], gates, cwin, cache, lw["cw"], ov_s, expand, layer=l,
                               n_past=n_past)
        o = jnp.concatenate([o16[:, :Q_PER_KV, :HEAD_DIM], o16[:, Q_PER_KV:, HEAD_DIM:]], 1).reshape(n_dec, D_ATTN)
        xr, cnt = _merge_call(xs, ya, sgb, o, lw["wpb"], lw["wo"], lw["ln1_g"], lw["ln1_b"], wrt, br, tri_s,
                              tm=tm_s, alpha=alpha, o_transposed=False)
        xs = _moe(xr, cnt, lw, tm=tm_s, tmx=tmx_s, alpha=alpha)
        kvr = kv.reshape(n_dec, 1, 6, KV_HEADS, HEAD_DIM)
        kv_s.append(kvr[:, :, :N_KV_SLOTS])
        win_s.append(jnp.concatenate([cache_win[:, l], kvr[:, :, N_KV_SLOTS:]], axis=1)[:, -wb:])
        gv_s.append(vn[:, None, :])
    return (xp.reshape(batch, seq, D_MODEL), xs.reshape(n_dec, 1, D_MODEL), jnp.stack(kv_p, axis=1),
            jnp.stack(win_p, axis=1), jnp.stack(kv_s, axis=1), jnp.stack(win_s, axis=1), jnp.stack(gv_s, axis=1))
```

```python
import functools

import numpy as np
import jax
import jax.numpy as jnp
from jax import lax
from jax.experimental import pallas as pl
from jax.experimental.pallas import tpu as pltpu

D_MODEL = 1024
CHUNK = 128
D_GMLP = 1024
GMLP_GROUPS = 8
N_HEADS = 16
KV_HEADS = 2
Q_PER_KV = N_HEADS // KV_HEADS
HEAD_DIM = 64
D_ATTN = N_HEADS * HEAD_DIM
D_KV = KV_HEADS * HEAD_DIM
CMP_BLOCK = 32
CMP_STRIDE = 16
CMP_HIDDEN = 2 * HEAD_DIM
SEL_BLOCK = 64
N_SELECT = 16
WINDOW = 512
Q_BLOCK = 128
FORCE_BONUS = 1e4
N_KV_SLOTS = 4
N_EXPERTS = 16
N_EXPERT_GROUPS = 4
EXPERTS_PER_GROUP = 4
D_EXPERT = 512
PAGE_SIZE = 128
PROJ_SIZES = (D_GMLP, D_GMLP, D_ATTN) + (D_KV,) * 6 + (3 * N_HEADS, D_MODEL, D_MODEL)

LANES = 128
SUBLANES = 8
VMEM_LIMIT_BYTES = 56 * 1024 * 1024

NEG = -1e30
LOG2E = 1.4426950408889634
KEY_TILE = 2 * Q_BLOCK
COL_CHUNK = 2 * LANES
ONES_ROWS = 16
BF = jnp.bfloat16
F32 = jnp.float32
ROUTE_W = LANES
XR_W = D_MODEL + ROUTE_W
ROW_DMA_UNROLL = 8


def _cparams(sem):
    return pltpu.CompilerParams(dimension_semantics=sem, vmem_limit_bytes=VMEM_LIMIT_BYTES)


def _layer_norm(x, g, b, eps=1e-5):
    mu = jnp.mean(x, -1, keepdims=True)
    xc = x - mu
    var = jnp.mean(xc * xc, -1, keepdims=True)
    return xc * lax.rsqrt(var + eps) * g + b


def _dot(a, b):
    return jnp.dot(a, b, preferred_element_type=F32)


def _dot_nt(a, b):
    return lax.dot_general(a, b, (((1,), (1,)), ((), ())), preferred_element_type=F32)


def _dot_tn(a, b):
    return lax.dot_general(a, b, (((0,), (0,)), ((), ())), preferred_element_type=F32)


def _full(shape):
    nd = len(shape)
    return pl.BlockSpec(shape, lambda *_: (0,) * nd)


def _gmlp_kernel(x_ref, wu_ref, wv_ref, wga_ref, wpa_ref, lng_ref, lnb_ref, ws_ref, bs_ref,
                 ya_ref, *rest, sample):
    xb = x_ref[...].astype(BF)
    u = jax.nn.gelu(_dot(xb, wu_ref[...]))
    v = jax.nn.gelu(_dot(xb, wv_ref[...]))
    vn = _layer_norm(v, lng_ref[...], lnb_ref[...])
    tm = xb.shape[0]
    if sample:
        a = u * (vn * ws_ref[...] + bs_ref[...])
        rest[0][...] = vn
    else:
        n_chunk = tm // CHUNK
        vb = vn.astype(BF)
        row = lax.broadcasted_iota(jnp.int32, (CHUNK, CHUNK), 0)
        col = lax.broadcasted_iota(jnp.int32, (CHUNK, CHUNK), 1)
        causal = col <= row
        bs = bs_ref[...]
        cols = []
        for g in range(GMLP_GROUPS):
            lo = g * CHUNK
            wg = jnp.where(causal, ws_ref[g], 0.0).astype(BF)
            vg = jnp.concatenate([vb[c * CHUNK:(c + 1) * CHUNK, lo:lo + CHUNK] for c in range(n_chunk)], axis=1)
            mixed = _dot(wg, vg) + bs[:, g:g + 1]
            cols.append(jnp.concatenate([mixed[:, c * CHUNK:(c + 1) * CHUNK] for c in range(n_chunk)], axis=0))
        a = u * jnp.concatenate(cols, axis=1)
    ga = jax.nn.sigmoid(_dot(xb, wga_ref[...]))
    ya_ref[...] = ga * _dot(a.astype(BF), wpa_ref[...])


def _gmlp_call(x, wu, wv, wga, wpa, lng, lnb, ws, bs, *, tm, sample):
    n = x.shape[0]
    wspec = _full((D_MODEL, D_MODEL))
    vspec = _full((1, D_MODEL))
    in_specs = [pl.BlockSpec((tm, D_MODEL), lambda i: (i, 0)), wspec, wspec, wspec, wspec, vspec, vspec,
                _full(ws.shape), _full(bs.shape)]
    tok = pl.BlockSpec((tm, D_MODEL), lambda i: (i, 0))
    out_shape = [jax.ShapeDtypeStruct((n, D_MODEL), F32)]
    out_specs = [tok]
    if sample:
        out_shape.append(jax.ShapeDtypeStruct((n, D_MODEL), F32))
        out_specs.append(tok)
    return pl.pallas_call(
        functools.partial(_gmlp_kernel, sample=sample),
        grid=(n // tm,), in_specs=in_specs, out_specs=out_specs, out_shape=out_shape,
        compiler_params=_cparams(("arbitrary",)), name="gmlp_sample" if sample else "gmlp_prompt",
    )(x, wu, wv, wga, wpa, lng, lnb, ws, bs)


def _attn_in_prompt_kernel(x_ref, wgb_ref, wkv_ref, wqt_ref, wkvt_ref, wgnt_ref,
                           sgb_ref, kv_ref, kvb_ref, kvt_ref, qt_ref, gnt_ref):
    xb = x_ref[...].astype(BF)
    sgb_ref[...] = jax.nn.sigmoid(_dot(xb, wgb_ref[...]))
    kv = _dot(xb, wkv_ref[...])
    kv_ref[...] = kv
    kvb_ref[...] = kv.astype(BF)
    kvt_ref[...] = _dot_nt(wkvt_ref[...], xb).astype(BF)
    qt_ref[...] = (_dot_nt(wqt_ref[...], xb) * (LOG2E * HEAD_DIM ** -0.5)).astype(BF)
    gnt_ref[...] = jax.nn.sigmoid(_dot_nt(wgnt_ref[...], xb))


def _attn_in_prompt_call(x, wgb, wkv, wqt, wkvt, wgnt, *, tm):
    n = x.shape[0]
    nkv = wkv.shape[1]
    ngn = wgnt.shape[0]
    in_specs = [pl.BlockSpec((tm, D_MODEL), lambda i: (i, 0)), _full(wgb.shape), _full(wkv.shape),
                _full(wqt.shape), _full(wkvt.shape), _full(wgnt.shape)]
    out_shape = [jax.ShapeDtypeStruct((n, D_MODEL), F32), jax.ShapeDtypeStruct((n, nkv), F32),
                 jax.ShapeDtypeStruct((n, nkv), BF), jax.ShapeDtypeStruct((nkv, n), BF),
                 jax.ShapeDtypeStruct((D_ATTN, n), BF), jax.ShapeDtypeStruct((ngn, n), F32)]
    out_specs = [pl.BlockSpec((tm, D_MODEL), lambda i: (i, 0)), pl.BlockSpec((tm, nkv), lambda i: (i, 0)),
                 pl.BlockSpec((tm, nkv), lambda i: (i, 0)), pl.BlockSpec((nkv, tm), lambda i: (0, i)),
                 pl.BlockSpec((D_ATTN, tm), lambda i: (0, i)), pl.BlockSpec((ngn, tm), lambda i: (0, i))]
    return pl.pallas_call(
        _attn_in_prompt_kernel, grid=(n // tm,), in_specs=in_specs, out_specs=out_specs, out_shape=out_shape,
        compiler_params=_cparams(("arbitrary",)), name="attn_in_prompt",
    )(x, wgb, wkv, wqt, wkvt, wgnt)


def _attn_in_sample_kernel(x_ref, wgb_ref, wkv_ref, wq_ref, wgn_ref, sgb_ref, kv_ref, q_ref, gn_ref):
    xb = x_ref[...].astype(BF)
    sgb_ref[...] = jax.nn.sigmoid(_dot(xb, wgb_ref[...]))
    kv_ref[...] = _dot(xb, wkv_ref[...])
    q_ref[...] = _dot(xb, wq_ref[...]) * (HEAD_DIM ** -0.5)
    gn_ref[...] = jax.nn.sigmoid(_dot(xb, wgn_ref[...]))


def _attn_in_sample_call(x, wgb, wkv, wq, wgn):
    n = x.shape[0]
    nkv = wkv.shape[1]
    out_shape = [jax.ShapeDtypeStruct((n, D_MODEL), F32), jax.ShapeDtypeStruct((n, nkv), F32),
                 jax.ShapeDtypeStruct((n, D_ATTN), F32), jax.ShapeDtypeStruct((n, wgn.shape[1]), F32)]
    return pl.pallas_call(
        _attn_in_sample_kernel, grid=(1,),
        in_specs=[_full(x.shape), _full(wgb.shape), _full(wkv.shape), _full(wq.shape), _full(wgn.shape)],
        out_specs=[_full(s.shape) for s in out_shape], out_shape=out_shape,
        compiler_params=_cparams(("arbitrary",)), name="attn_in_sample",
    )(x, wgb, wkv, wq, wgn)


def _compress_rows(src_refs, n_sub, wp_ref, w2p_ref, w2pt_ref, ppos_ref):
    outs = []
    for slot in range(2):
        src_ref = src_refs[slot]
        x = jnp.concatenate(
            [src_ref[pl.ds(s, n_sub, stride=CMP_STRIDE), :].astype(BF) for s in range(CMP_STRIDE)],
            axis=1)
        hh = _dot(x, wp_ref[slot])
        pp = _dot(ppos_ref[slot], wp_ref[slot])
        pos = pp[0:1, 0:2 * CMP_HIDDEN] + pp[1:2, 2 * CMP_HIDDEN:]
        h1 = pltpu.roll(hh[:, 2 * CMP_HIDDEN:], n_sub - 1, 0)
        g = jax.nn.gelu(hh[:, 0:2 * CMP_HIDDEN] + h1 + pos).astype(BF)
        if slot == 0:
            outs.append(_dot(g, w2p_ref[0]))
        else:
            outs.append(_dot_nt(w2pt_ref[1], g))
    return outs


def _compress_kernel(kcm_ref, vcm_ref, wp_ref, w2p_ref, w2pt_ref, ppos_ref, kc_ref, vct_ref):
    n_sub = kc_ref.shape[0]
    kc, vct = _compress_rows((kcm_ref, vcm_ref), n_sub, wp_ref, w2p_ref, w2pt_ref, ppos_ref)
    kc_ref[...] = kc.astype(BF)
    vct_ref[...] = vct.astype(BF)


def _compress_call(kv, cw, *, batch, seq):
    n_sub = seq // CMP_STRIDE
    wp, w2p, w2pt, ppos = cw
    return pl.pallas_call(
        _compress_kernel, grid=(batch,),
        in_specs=[pl.BlockSpec((seq, LANES), lambda b: (b, 0)), pl.BlockSpec((seq, LANES), lambda b: (b, 1)),
                  _full(wp.shape), _full(w2p.shape), _full(w2pt.shape), _full(ppos.shape)],
        out_specs=[pl.BlockSpec((None, n_sub, LANES), lambda b: (b, 0, 0)),
                   pl.BlockSpec((None, LANES, n_sub), lambda b: (b, 0, 0))],
        out_shape=[jax.ShapeDtypeStruct((batch, n_sub, LANES), BF), jax.ShapeDtypeStruct((batch, LANES, n_sub), BF)],
        compiler_params=_cparams(("arbitrary",)), name="compress_prompt",
    )(kv, kv, wp, w2p, w2pt, ppos)


def _top_blocks(score):
    n_blk = score.shape[0]
    jf = lax.broadcasted_iota(jnp.int32, score.shape, 0).astype(F32)
    work = score
    sel = jnp.zeros_like(score)
    for _ in range(N_SELECT):
        mx = jnp.max(work, axis=0, keepdims=True)
        first = jnp.min(jnp.where(work == mx, jf, float(n_blk)), axis=0, keepdims=True)
        pick = jf == first
        sel = jnp.where(pick, 1.0, sel)
        work = jnp.where(pick, -3e38, work)
    return sel


def _nsa_prompt_kernel(qt_ref, gn_ref, ksel_ref, kwin_ref, vselt_ref, vwint_ref, kc_ref, vct_ref, ovt_ref,
                       o_ref, sel_ref, m_ref, acc_ref, qtp_ref, tot_ref, sa_ref, sb_ref):
    i = pl.program_id(1)
    q0 = i * Q_BLOCK
    seq = ksel_ref.shape[0]
    n_cmp = kc_ref.shape[0]
    n_selb = ovt_ref.shape[0]
    nq = Q_PER_KV * Q_BLOCK
    qpos = q0 + lax.broadcasted_iota(jnp.int32, (1, Q_BLOCK), 1)
    cmp_end = lax.broadcasted_iota(jnp.int32, (n_cmp, Q_BLOCK), 0) * CMP_STRIDE + (CMP_BLOCK - 1)
    vis = cmp_end <= qpos
    blk = lax.broadcasted_iota(jnp.int32, (n_selb, Q_BLOCK), 0)
    cur = qpos // SEL_BLOCK
    forced = (blk == 0) | (blk == cur) | (blk == cur - 1)
    valid = blk * SEL_BLOCK <= qpos
    krow = lax.broadcasted_iota(jnp.int32, (KEY_TILE, Q_BLOCK), 0)
    tcol = lax.broadcasted_iota(jnp.int32, (KEY_TILE, Q_BLOCK), 1)
    zeros_q = jnp.zeros((HEAD_DIM, nq), BF)
    ones = jnp.ones((ONES_ROWS, KEY_TILE), BF)

    chunks = [(h, c) for h in range(KV_HEADS) for c in range(nq // COL_CHUNK)]

    def chunk_cols(c):
        return slice(c * COL_CHUNK, (c + 1) * COL_CHUNK)

    def score_chunk(dst_ref, k_tile, h, c):
        dst_ref[h, :, chunk_cols(c)] = _dot(k_tile, qtp_ref[h, :, chunk_cols(c)])

    def flash_tile(src_ref, dst_ref, k_next, vt_tiles, masks):
        vt_augs = [jnp.concatenate([vt, ones], axis=0) for vt in vt_tiles]
        for h, c in chunks:
            cols = chunk_cols(c)
            score_chunk(dst_ref, k_next, h, c)
            m_old = m_ref[h, :, cols]
            ps, m_news = [], []
            for g in range(COL_CHUNK // Q_BLOCK):
                lo = c * COL_CHUNK + g * Q_BLOCK
                s = jnp.where(masks[h], src_ref[h, :, lo:lo + Q_BLOCK], NEG)
                m_new = jnp.maximum(m_old[:, g * Q_BLOCK:(g + 1) * Q_BLOCK], jnp.max(s, axis=0, keepdims=True))
                ps.append(jnp.exp2(s - m_new).astype(BF))
                m_news.append(m_new)
            m_new = jnp.concatenate(m_news, axis=1)
            acc_ref[h, :, cols] = (jnp.exp2(m_old - m_new) * acc_ref[h, :, cols]
                                   + _dot(vt_augs[h], jnp.concatenate(ps, axis=1)))
            m_ref[h, :, cols] = m_new

    def flash_branch(k_ref, vt_ref, n_tiles, tile_start, tile_masks):
        def k_tile(t):
            k0 = pl.multiple_of(jnp.clip(tile_start(t), 0, seq - KEY_TILE), Q_BLOCK)
            return k_ref[pl.ds(k0, KEY_TILE), :]

        def vt_tiles(t):
            k0 = pl.multiple_of(jnp.clip(tile_start(t), 0, seq - KEY_TILE), Q_BLOCK)
            return [head_rows(vt_ref, h)[:, pl.ds(k0, KEY_TILE)] for h in range(KV_HEADS)]

        m_ref[...] = jnp.full(m_ref.shape, NEG, F32)
        acc_ref[...] = jnp.zeros(acc_ref.shape, F32)
        first = k_tile(0)
        for h, c in chunks:
            score_chunk(sa_ref, first, h, c)

        def body(j, carry):
            flash_tile(sa_ref, sb_ref, k_tile(2 * j + 1), vt_tiles(2 * j), tile_masks(2 * j))
            flash_tile(sb_ref, sa_ref, k_tile(2 * j + 2), vt_tiles(2 * j + 1), tile_masks(2 * j + 1))
            return carry

        lax.fori_loop(0, (n_tiles + 1) // 2, body, 0)

    def head_rows(ref, h):
        return ref.at[h * HEAD_DIM:(h + 1) * HEAD_DIM]

    def flash_result(h):
        return acc_ref[h, 0:HEAD_DIM, :] * (1.0 / acc_ref[h, HEAD_DIM:HEAD_DIM + 1, :])

    def gate(h, c):
        r = (h * 3 + c) * Q_PER_KV
        return jnp.concatenate([gn_ref[r + g:r + g + 1, :] for g in range(Q_PER_KV)], axis=1)

    scores = []
    for h in range(KV_HEADS):
        qh = qt_ref[h * Q_PER_KV * HEAD_DIM:(h + 1) * Q_PER_KV * HEAD_DIM, :]
        qcat = jnp.concatenate([qh[g * HEAD_DIM:(g + 1) * HEAD_DIM, :] for g in range(Q_PER_KV)], axis=1)
        qtp = jnp.concatenate([qcat, zeros_q] if h == 0 else [zeros_q, qcat], axis=0)
        qtp_ref[h] = qtp

        sc = _dot(kc_ref[...], qtp)
        psum = jnp.zeros((n_cmp, Q_BLOCK), F32)
        pcols = []
        for g in range(Q_PER_KV):
            s = jnp.where(vis, sc[:, g * Q_BLOCK:(g + 1) * Q_BLOCK], NEG)
            mx = jnp.max(s, axis=0, keepdims=True)
            p = jnp.where(vis, jnp.exp2(s - mx), 0.0)
            den = jnp.maximum(jnp.sum(p, axis=0, keepdims=True), 1e-30)
            p = p * (1.0 / den)
            psum = psum + p
            pcols.append(p.astype(BF))
        tot_ref[h] = gate(h, 0) * _dot(vct_ref[h * HEAD_DIM:(h + 1) * HEAD_DIM, :], jnp.concatenate(pcols, axis=1))

        p_hi = psum.astype(BF)
        p_lo = (psum - p_hi.astype(F32)).astype(BF)
        imp = _dot(ovt_ref[...], p_hi) + _dot(ovt_ref[...], p_lo)
        scores.append(jnp.where(valid, imp + jnp.where(forced, FORCE_BONUS, 0.0), NEG))

    sel_ref[...] = _top_blocks(jnp.concatenate(scores, axis=1))

    blocks_per_tile = KEY_TILE // SEL_BLOCK

    def sel_masks(t):
        causal = t * KEY_TILE + krow <= q0 + tcol
        rows = [sel_ref[pl.ds(jnp.minimum(blocks_per_tile * t + j, n_selb - 1), 1), :]
                for j in range(blocks_per_tile)]
        return [causal & (jnp.concatenate(
            [jnp.broadcast_to(r[:, h * Q_BLOCK:(h + 1) * Q_BLOCK], (SEL_BLOCK, Q_BLOCK)) for r in rows], axis=0) > 0.5)
            for h in range(KV_HEADS)]

    flash_branch(ksel_ref, vselt_ref, (q0 + Q_BLOCK + KEY_TILE - 1) // KEY_TILE, lambda t: t * KEY_TILE, sel_masks)
    for h in range(KV_HEADS):
        tot_ref[h] = tot_ref[h] + gate(h, 1) * flash_result(h)

    n_wt = (WINDOW + Q_BLOCK + KEY_TILE - 1) // KEY_TILE
    w0 = jnp.clip(q0 - WINDOW, 0, seq - n_wt * KEY_TILE)

    def win_masks(t):
        rel = (q0 + tcol) - (w0 + t * KEY_TILE + krow)
        return [(rel >= 0) & (rel < WINDOW)] * KV_HEADS

    flash_branch(kwin_ref, vwint_ref, n_wt, lambda t: w0 + t * KEY_TILE, win_masks)

    for h in range(KV_HEADS):
        tot = tot_ref[h] + gate(h, 2) * flash_result(h)
        for g in range(Q_PER_KV):
            r = (h * Q_PER_KV + g) * HEAD_DIM
            o_ref[r:r + HEAD_DIM, :] = tot[:, g * Q_BLOCK:(g + 1) * Q_BLOCK].astype(BF)


def _nsa_prompt_call(qt, gnt, kvb, kvt, kc, vct, ovt, *, batch, seq):
    nblk = seq // Q_BLOCK
    n = batch * seq
    n_sub = seq // CMP_STRIDE
    n_selb = seq // SEL_BLOCK
    nq = Q_PER_KV * Q_BLOCK
    in_specs = [
        pl.BlockSpec((D_ATTN, Q_BLOCK), lambda b, i: (0, b * nblk + i)),
        pl.BlockSpec((gnt.shape[0], Q_BLOCK), lambda b, i: (0, b * nblk + i)),
        pl.BlockSpec((seq, LANES), lambda b, i: (b, 2)),
        pl.BlockSpec((seq, LANES), lambda b, i: (b, 4)),
        pl.BlockSpec((LANES, seq), lambda b, i: (3, b)),
        pl.BlockSpec((LANES, seq), lambda b, i: (5, b)),
        pl.BlockSpec((None, n_sub, LANES), lambda b, i: (b, 0, 0)),
        pl.BlockSpec((None, LANES, n_sub), lambda b, i: (b, 0, 0)),
        _full(ovt.shape),
    ]
    return pl.pallas_call(
        _nsa_prompt_kernel, grid=(batch, nblk), in_specs=in_specs,
        out_specs=pl.BlockSpec((D_ATTN, Q_BLOCK), lambda b, i: (0, b * nblk + i)),
        out_shape=jax.ShapeDtypeStruct((D_ATTN, n), BF),
        scratch_shapes=[pltpu.VMEM((n_selb, KV_HEADS * Q_BLOCK), F32), pltpu.VMEM((KV_HEADS, 1, nq), F32),
                        pltpu.VMEM((KV_HEADS, HEAD_DIM + ONES_ROWS, nq), F32), pltpu.VMEM((KV_HEADS, LANES, nq), BF),
                        pltpu.VMEM((KV_HEADS, HEAD_DIM, nq), F32),
                        pltpu.VMEM((KV_HEADS, KEY_TILE, nq), F32), pltpu.VMEM((KV_HEADS, KEY_TILE, nq), F32)],
        compiler_params=_cparams(("arbitrary", "arbitrary")), name="nsa_prompt",
    )(qt, gnt, kvb, kvb, kvt, kvt, kc, vct, ovt)


def _route(x1, wrt_ref, br_ref, tri_ref, cnt_ref):
    tm = x1.shape[0]
    logits = lax.dot_general(wrt_ref[...], x1, (((1,), (1,)), ((), ())), precision=lax.Precision.HIGHEST,
                             preferred_element_type=F32)
    aff = jax.nn.sigmoid(logits)
    grp = aff + br_ref[...]
    affr = [aff[k:k + 1, :] for k in range(N_EXPERTS)]
    grpr = [grp[k:k + 1, :] for k in range(N_EXPERTS)]
    best = None
    gsel = jnp.zeros((1, tm), jnp.int32)
    for gi in range(N_EXPERT_GROUPS):
        m = grpr[gi * 4:(gi + 1) * 4]
        top2 = None
        for a in range(4):
            for b in range(a + 1, 4):
                s = m[a] + m[b]
                top2 = s if top2 is None else jnp.maximum(top2, s)
        if best is None:
            best = top2
        else:
            better = top2 > best
            gsel = jnp.where(better, gi, gsel)
            best = jnp.where(better, top2, best)
    ing, ina = [], []
    for j in range(4):
        vg, va = grpr[j], affr[j]
        for gi in range(1, N_EXPERT_GROUPS):
            vg = jnp.where(gsel == gi, grpr[gi * 4 + j], vg)
            va = jnp.where(gsel == gi, affr[gi * 4 + j], va)
        ing.append(vg)
        ina.append(va)

    def argmax4(vals):
        bv, bi = vals[0], jnp.zeros((1, tm), jnp.int32)
        for j in range(1, 4):
            better = vals[j] > bv
            bi = jnp.where(better, j, bi)
            bv = jnp.where(better, vals[j], bv)
        return bi

    loc1 = argmax4(ing)
    loc2 = argmax4([jnp.where(loc1 == j, -3e38, ing[j]) for j in range(4)])
    w1 = sum(jnp.where(loc1 == j, ina[j], 0.0) for j in range(4))
    w2 = sum(jnp.where(loc2 == j, ina[j], 0.0) for j in range(4))
    tot = w1 + w2
    wd = [jnp.where(loc1 == j, w1 / tot, 0.0) + jnp.where(loc2 == j, w2 / tot, 0.0) for j in range(4)]
    oh = jnp.concatenate([(gsel == gi).astype(F32) for gi in range(N_EXPERT_GROUPS)]
                         + [jnp.zeros((SUBLANES - N_EXPERT_GROUPS, tm), F32)], axis=0)
    cum = _dot(oh.astype(BF), tri_ref[...])
    carry = cnt_ref[...][:, 0:1]
    rank = jnp.sum(oh * (cum + carry), axis=0, keepdims=True)
    cnt_ref[...] = cnt_ref[...] + jnp.sum(oh, axis=1, keepdims=True)
    rows = wd + [gsel.astype(F32), rank, jnp.zeros((ROUTE_W - 6, tm), F32)]
    return jnp.concatenate(rows, axis=0)


def _merge_kernel(x_ref, ya_ref, sgb_ref, o_ref, wpb_ref, wo_ref, g_ref, b_ref, wrt_ref, br_ref, tri_ref,
                  xr_ref, cnt_out_ref, cnt_ref, *, alpha, o_transposed):
    @pl.when(pl.program_id(0) == 0)
    def _():
        cnt_ref[...] = jnp.zeros(cnt_ref.shape, F32)

    if o_transposed:
        ob = _dot_tn(o_ref[...], wpb_ref[...])
    else:
        ob = _dot(o_ref[...].astype(BF), wpb_ref[...])
    y = ya_ref[...] + sgb_ref[...] * ob
    mix = _dot(y.astype(BF), wo_ref[...])
    x1 = _layer_norm(alpha * x_ref[...] + mix, g_ref[...], b_ref[...])
    xr_ref[:, 0:D_MODEL] = x1
    info = _route(x1, wrt_ref, br_ref, tri_ref, cnt_ref)
    xr_ref[:, D_MODEL:XR_W] = info.T
    cnt_out_ref[...] = cnt_ref[...]


def _merge_call(x, ya, sgb, o, wpb, wo, g, b, wrt, br, tri, *, tm, alpha, o_transposed):
    n = ya.shape[0]
    tok = pl.BlockSpec((tm, D_MODEL), lambda i: (i, 0))
    ospec = pl.BlockSpec((D_ATTN, tm), lambda i: (0, i)) if o_transposed else tok
    return pl.pallas_call(
        functools.partial(_merge_kernel, alpha=alpha, o_transposed=o_transposed), grid=(n // tm,),
        in_specs=[tok, tok, tok, ospec, _full(wpb.shape), _full(wo.shape), _full(g.shape), _full(b.shape),
                  _full(wrt.shape), _full(br.shape), _full(tri.shape)],
        out_specs=[pl.BlockSpec((tm, XR_W), lambda i: (i, 0)), _full((SUBLANES, LANES))],
        out_shape=[jax.ShapeDtypeStruct((n, XR_W), F32), jax.ShapeDtypeStruct((SUBLANES, LANES), F32)],
        scratch_shapes=[pltpu.VMEM((SUBLANES, LANES), F32)],
        compiler_params=_cparams(("arbitrary",)), name="merge_t" if o_transposed else "merge_n",
    )(x, ya, sgb, o, wpb, wo, g, b, wrt, br, tri)


def _row_copy(src_ref, dst_ref, sem, src_row, dst_row):
    return pltpu.make_async_copy(src_ref.at[pl.ds(src_row, 1), :], dst_ref.at[pl.ds(dst_row, 1), :], sem)


def _scatter_kernel(dest_ref, xr_ref, xs_in_ref, xs_ref, sem):
    del xs_in_ref
    tm = xr_ref.shape[0]

    def start(r, c):
        _row_copy(xr_ref, xs_ref, sem, r, dest_ref[r]).start()
        return c

    def wait(r, c):
        _row_copy(xr_ref, xs_ref, sem, r, dest_ref[r]).wait()
        return c

    lax.fori_loop(0, tm, start, 0, unroll=ROW_DMA_UNROLL)
    lax.fori_loop(0, tm, wait, 0, unroll=ROW_DMA_UNROLL)


def _scatter_call(dest, xr, xs0, *, tm):
    n = xr.shape[0]
    return pl.pallas_call(
        _scatter_kernel, grid=(n // tm,),
        in_specs=[pl.BlockSpec((tm,), lambda i: (i,), memory_space=pltpu.SMEM),
                  pl.BlockSpec((tm, XR_W), lambda i: (i, 0)), pl.BlockSpec(memory_space=pl.ANY)],
        out_specs=pl.BlockSpec(memory_space=pl.ANY),
        out_shape=jax.ShapeDtypeStruct(xs0.shape, F32),
        scratch_shapes=[pltpu.SemaphoreType.DMA(())],
        input_output_aliases={2: 0},
        compiler_params=_cparams(("arbitrary",)), name="moe_scatter",
    )(dest, xr, xs0)


def _experts_kernel(tg_ref, nu_ref, xs_ref, wg_ref, wu_ref, wd_ref, ys_ref):
    del tg_ref
    j = pl.program_id(0)

    @pl.when(j < nu_ref[0])
    def _():
        xb = xs_ref[:, 0:D_MODEL].astype(BF)
        acc = None
        for e in range(EXPERTS_PER_GROUP):
            hidden = jax.nn.silu(_dot(xb, wg_ref[e])) * _dot(xb, wu_ref[e])
            hidden = hidden * xs_ref[:, D_MODEL + e:D_MODEL + e + 1]
            y = _dot(hidden.astype(BF), wd_ref[e])
            acc = y if acc is None else acc + y
        ys_ref[...] = acc

    @pl.when(j >= nu_ref[0])
    def _():
        ys_ref[...] = jnp.zeros(ys_ref.shape, F32)


def _experts_call(tile_group, n_used, xs, wg, wu, wd, *, tmx):
    rows = xs.shape[0]
    wspec_in = pl.BlockSpec((None, EXPERTS_PER_GROUP, D_MODEL, D_EXPERT), lambda j, tg, nu: (tg[j], 0, 0, 0))
    wspec_out = pl.BlockSpec((None, EXPERTS_PER_GROUP, D_EXPERT, D_MODEL), lambda j, tg, nu: (tg[j], 0, 0, 0))
    grid_spec = pltpu.PrefetchScalarGridSpec(
        num_scalar_prefetch=2, grid=(rows // tmx,),
        in_specs=[pl.BlockSpec((tmx, XR_W), lambda j, tg, nu: (j, 0)), wspec_in, wspec_in, wspec_out],
        out_specs=pl.BlockSpec((tmx, D_MODEL), lambda j, tg, nu: (j, 0)))
    return pl.pallas_call(
        _experts_kernel, grid_spec=grid_spec, out_shape=jax.ShapeDtypeStruct((rows, D_MODEL), F32),
        compiler_params=_cparams(("arbitrary",)), name="moe_experts",
    )(tile_group, n_used, xs, wg, wu, wd)


def _combine_kernel(dest_ref, xr_ref, ys_ref, g_ref, b_ref, x2_ref, buf_ref, sem, *, alpha):
    tm = xr_ref.shape[0]

    def start(r, c):
        _row_copy(ys_ref, buf_ref, sem, dest_ref[r], r).start()
        return c

    def wait(r, c):
        _row_copy(ys_ref, buf_ref, sem, dest_ref[r], r).wait()
        return c

    lax.fori_loop(0, tm, start, 0, unroll=ROW_DMA_UNROLL)
    lax.fori_loop(0, tm, wait, 0, unroll=ROW_DMA_UNROLL)
    x2_ref[...] = _layer_norm(alpha * xr_ref[...] + buf_ref[...], g_ref[...], b_ref[...])


def _combine_call(dest, xr, ys, g, b, *, tm, alpha):
    n = xr.shape[0]
    return pl.pallas_call(
        functools.partial(_combine_kernel, alpha=alpha), grid=(n // tm,),
        in_specs=[pl.BlockSpec((tm,), lambda i: (i,), memory_space=pltpu.SMEM),
                  pl.BlockSpec((tm, D_MODEL), lambda i: (i, 0)), pl.BlockSpec(memory_space=pl.ANY),
                  _full(g.shape), _full(b.shape)],
        out_specs=pl.BlockSpec((tm, D_MODEL), lambda i: (i, 0)),
        out_shape=jax.ShapeDtypeStruct((n, D_MODEL), F32),
        scratch_shapes=[pltpu.VMEM((tm, D_MODEL), F32), pltpu.SemaphoreType.DMA(())],
        compiler_params=_cparams(("arbitrary",)), name="moe_combine",
    )(dest, xr, ys, g, b)


def _moe(xr, cnt, lw, *, tm, tmx, alpha):
    n = xr.shape[0]
    n_tiles = n // tmx + N_EXPERT_GROUPS
    counts = cnt[:N_EXPERT_GROUPS, 0].astype(jnp.int32)
    tiles_per = (counts + tmx - 1) // tmx
    tile_end = jnp.cumsum(tiles_per)
    offs = (tile_end - tiles_per) * tmx
    gid = xr[:, D_MODEL + 4].astype(jnp.int32)
    rank = xr[:, D_MODEL + 5].astype(jnp.int32)
    dest = offs[gid] + rank
    tile_group = jnp.minimum(jnp.sum(jnp.arange(n_tiles)[:, None] >= tile_end[None, :], axis=1),
                             N_EXPERT_GROUPS - 1).astype(jnp.int32)
    n_used = tile_end[-1:].astype(jnp.int32)
    xs = _scatter_call(dest, xr, jnp.zeros((n_tiles * tmx, XR_W), F32), tm=tm)
    ys = _experts_call(tile_group, n_used, xs, lw["wg"], lw["wu_e"], lw["wd"], tmx=tmx)
    return _combine_call(dest, xr, ys, lw["ln2_g"], lw["ln2_b"], tm=tm, alpha=alpha)


def _softmax_with_new(s, mask, s_new, new_on):
    s = jnp.where(mask, s, NEG)
    s_new = jnp.where(new_on, s_new, NEG)
    mx = jnp.maximum(jnp.max(s, axis=1, keepdims=True), s_new)
    p = jnp.where(mask, jnp.exp(s - mx), 0.0)
    p_new = jnp.where(new_on, jnp.exp(s_new - mx), 0.0)
    den = jnp.maximum(jnp.sum(p, axis=1, keepdims=True) + p_new, 1e-30)
    return p, p_new, 1.0 / den


def _nsa_sample_kernel(pt_ref, q_ref, new_ref, gate_ref, cw_ref, cache_ref, wp_ref, w2p_ref, w2pt_ref, ppos_ref,
                       ov_ref, exp_ref, o_ref, buf_ref, sem, *, layer, n_pages, n_past):
    b = pl.program_id(0)
    n_sub = n_past // CMP_STRIDE
    n_cmp = n_sub - CMP_BLOCK // CMP_STRIDE + 1
    nrow = KV_HEADS * Q_PER_KV

    def page_copy(p, slot):
        return pltpu.make_async_copy(
            cache_ref.at[pt_ref[b * n_pages + p], layer, :, pl.ds(slot * LANES, LANES)],
            buf_ref.at[slot, pl.ds(p * PAGE_SIZE, PAGE_SIZE), :], sem)

    for p in range(n_pages):
        for slot in range(N_KV_SLOTS):
            page_copy(p, slot).start()
    for p in range(n_pages):
        for slot in range(N_KV_SLOTS):
            page_copy(p, slot).wait()

    q = q_ref[...]
    qf = q.astype(F32)
    new = new_ref[...]
    qpos = n_past

    def new_score(col):
        kn = new[:, col * LANES:(col + 1) * LANES].astype(BF).astype(F32)
        return jnp.sum(qf * kn, axis=1, keepdims=True)

    def new_value(col):
        return new[:, col * LANES:(col + 1) * LANES].astype(BF).astype(F32)

    kc, vct = _compress_rows((buf_ref.at[0], buf_ref.at[1]), n_sub, wp_ref, w2p_ref, w2pt_ref, ppos_ref)
    s_c = _dot_nt(q, kc.astype(BF))
    n_idx = lax.broadcasted_iota(jnp.int32, (nrow, n_sub), 1)
    vis = (n_idx * CMP_STRIDE + (CMP_BLOCK - 1) <= qpos) & (n_idx < n_cmp)
    s_c = jnp.where(vis, s_c, NEG)
    mx = jnp.max(s_c, axis=1, keepdims=True)
    p_c = jnp.where(vis, jnp.exp(s_c - mx), 0.0)
    p_c = p_c * (1.0 / jnp.maximum(jnp.sum(p_c, axis=1, keepdims=True), 1e-30))
    o_c = _dot_nt(p_c.astype(BF), vct.astype(BF))

    psum = jnp.concatenate([jnp.sum(p_c[h * Q_PER_KV:(h + 1) * Q_PER_KV], axis=0, keepdims=True)
                            for h in range(KV_HEADS)] + [jnp.zeros((SUBLANES - KV_HEADS, n_sub), F32)], axis=0)
    p_hi = psum.astype(BF)
    p_lo = (psum - p_hi.astype(F32)).astype(BF)
    imp = _dot(p_hi, ov_ref[...]) + _dot(p_lo, ov_ref[...])
    n_selp = imp.shape[1]
    j_idx = lax.broadcasted_iota(jnp.int32, (SUBLANES, n_selp), 1)
    cur = qpos // SEL_BLOCK
    forced = (j_idx == 0) | (j_idx == cur) | (j_idx == cur - 1)
    valid = j_idx * SEL_BLOCK <= qpos
    score = jnp.where(valid, imp + jnp.where(forced, FORCE_BONUS, 0.0), NEG)
    score_t = score.T
    jr = lax.broadcasted_iota(jnp.int32, (n_selp, n_selp), 0)
    jc = lax.broadcasted_iota(jnp.int32, (n_selp, n_selp), 1)
    sel_rows = []
    for h in range(KV_HEADS):
        other = jnp.broadcast_to(score_t[:, h:h + 1], (n_selp, n_selp))
        mine = jnp.broadcast_to(score[h:h + 1, :], (n_selp, n_selp))
        ahead = (other > mine) | ((other == mine) & (jr < jc))
        rank = jnp.sum(ahead.astype(F32), axis=0, keepdims=True)
        sel_rows.append(jnp.broadcast_to((rank < N_SELECT).astype(F32), (Q_PER_KV, n_selp)))
    sel = jnp.concatenate(sel_rows, axis=0)

    on = _dot(sel.astype(BF), exp_ref[...]) > 0.5
    ks = buf_ref[2].astype(BF)
    vs = buf_ref[3].astype(BF)
    new_on = sel[:, cur:cur + 1] > 0.5
    p_s, pn_s, inv_s = _softmax_with_new(_dot_nt(q, ks), on, new_score(2), new_on)
    o_s = (_dot(p_s.astype(BF), vs) + pn_s * new_value(3)) * inv_s

    wb = cw_ref.shape[0]
    kw = cw_ref[:, 0:LANES].astype(BF)
    vw = cw_ref[:, LANES:2 * LANES].astype(BF)
    wpos = (n_past - wb) + lax.broadcasted_iota(jnp.int32, (nrow, wb), 1)
    rel = qpos - wpos
    wmask = (rel >= 0) & (rel < WINDOW) & (wpos >= 0)
    p_w, pn_w, inv_w = _softmax_with_new(_dot_nt(q, kw), wmask, new_score(4), jnp.full((nrow, 1), True))
    o_w = (_dot(p_w.astype(BF), vw) + pn_w * new_value(5)) * inv_w

    gates = gate_ref[...]
    o_ref[...] = gates[:, 0:1] * o_c + gates[:, 1:2] * o_s + gates[:, 2:3] * o_w


def _nsa_sample_call(pt, qp, new, gates, cache_win, cache, cw, ov, expand, *, layer, n_past):
    n_dec = qp.shape[0]
    n_pages = n_past // PAGE_SIZE
    nrow = KV_HEADS * Q_PER_KV
    wb = cache_win.shape[2]
    wp, w2p, w2pt, ppos = cw
    const = lambda shape: pl.BlockSpec(shape, lambda b, pt_: (0,) * len(shape))
    grid_spec = pltpu.PrefetchScalarGridSpec(
        num_scalar_prefetch=1, grid=(n_dec,),
        in_specs=[pl.BlockSpec((None, nrow, LANES), lambda b, pt_: (b, 0, 0)),
                  pl.BlockSpec((None, 1, new.shape[2]), lambda b, pt_: (b, 0, 0)),
                  pl.BlockSpec((None, nrow, LANES), lambda b, pt_: (b, 0, 0)),
                  pl.BlockSpec((None, None, wb, 2 * LANES), lambda b, pt_: (b, layer, 0, 0)),
                  pl.BlockSpec(memory_space=pl.ANY),
                  const(wp.shape), const(w2p.shape), const(w2pt.shape), const(ppos.shape), const(ov.shape),
                  const(expand.shape)],
        out_specs=pl.BlockSpec((None, nrow, LANES), lambda b, pt_: (b, 0, 0)),
        scratch_shapes=[pltpu.VMEM((N_KV_SLOTS, n_past, LANES), F32), pltpu.SemaphoreType.DMA(())])
    return pl.pallas_call(
        functools.partial(_nsa_sample_kernel, layer=layer, n_pages=n_pages, n_past=n_past),
        grid_spec=grid_spec, out_shape=jax.ShapeDtypeStruct((n_dec, nrow, LANES), F32),
        compiler_params=_cparams(("arbitrary",)), name="nsa_sample",
    )(pt, qp, new, gates, cache_win, cache, wp, w2p, w2pt, ppos, ov, expand)


def _overlap(n_cmp, n_sel):
    c0 = np.arange(n_cmp) * CMP_STRIDE
    s0 = np.arange(n_sel) * SEL_BLOCK
    return ((c0[:, None] < s0[None, :] + SEL_BLOCK) & (c0[:, None] + CMP_BLOCK > s0[None, :])).astype(np.float32)


def _layer_weights(l, w_in, gmlp_ln_g, gmlp_ln_b, gmlp_ws, gmlp_bs, cmp_pos, cmp_w1, cmp_w2, w_branch_a, w_branch_b,
                   w_out, ln1_g, ln1_b, ln2_g, ln2_b, w_gate, w_up, w_down):
    cuts = np.cumsum((0,) + PROJ_SIZES)
    seg = lambda i, j=None: w_in[l][:, cuts[i]:cuts[i + 1 if j is None else j]].astype(BF)
    wgn = seg(9)
    eye = jnp.eye(KV_HEADS, dtype=F32)
    r = CMP_BLOCK // CMP_STRIDE
    w1r = cmp_w1[l].reshape(2, r, CMP_STRIDE, HEAD_DIM, CMP_HIDDEN)
    wp = jnp.einsum("xrsdf,hk->xshdrkf", w1r, eye).reshape(2, CMP_STRIDE * D_KV, r * KV_HEADS * CMP_HIDDEN)
    w2p = jnp.einsum("xfd,hk->xhfkd", cmp_w2[l], eye).reshape(2, KV_HEADS * CMP_HIDDEN, D_KV)
    pos = cmp_pos[l].reshape(2, r, CMP_STRIDE, 1, HEAD_DIM)
    ppos = jnp.broadcast_to(pos, (2, r, CMP_STRIDE, KV_HEADS, HEAD_DIM)).reshape(2, r, CMP_STRIDE * D_KV)
    ppos = jnp.pad(ppos, ((0, 0), (0, SUBLANES - r), (0, 0)))
    ex = lambda w: w.reshape((N_EXPERT_GROUPS, EXPERTS_PER_GROUP) + w.shape[1:]).astype(BF)
    row = lambda v: v[l][None, :].astype(F32)
    return {
        "wu": seg(0), "wv": seg(1), "wq": seg(2), "wkv": seg(3, 9), "wga": seg(10), "wgb": seg(11),
        "wqt": seg(2).T, "wkvt": seg(3, 9).T,
        "wgnt": wgn.reshape(D_MODEL, KV_HEADS, Q_PER_KV, 3).transpose(1, 3, 2, 0).reshape(3 * N_HEADS, D_MODEL),
        "wgn": jnp.pad(wgn, ((0, 0), (0, LANES - 3 * N_HEADS))),
        "gln_g": row(gmlp_ln_g), "gln_b": row(gmlp_ln_b),
        "ws": gmlp_ws[l], "bs_t": gmlp_bs[l].T,
        "ws0": jnp.repeat(gmlp_ws[l][:, 0, 0], CHUNK)[None, :], "bs0": jnp.repeat(gmlp_bs[l][:, 0], CHUNK)[None, :],
        "cw": (wp.astype(BF), w2p.astype(BF), w2p.transpose(0, 2, 1).astype(BF), ppos.astype(BF)),
        "wpa": w_branch_a[l].astype(BF), "wpb": w_branch_b[l].astype(BF), "wo": w_out[l].astype(BF),
        "ln1_g": row(ln1_g), "ln1_b": row(ln1_b), "ln2_g": row(ln2_g), "ln2_b": row(ln2_b),
        "wg": ex(w_gate[l]), "wu_e": ex(w_up[l]), "wd": ex(w_down[l]),
    }


def _strict_upper(n):
    return jnp.asarray(np.triu(np.ones((n, n), np.float32), 1), BF)


def kernel(x_prompt, x_sample, cache_kv, cache_win, page_table, w_in, gmlp_ln_g, gmlp_ln_b, gmlp_ws, gmlp_bs,
           cmp_pos, cmp_w1, cmp_w2, w_branch_a, w_branch_b, w_out, ln1_g, ln1_b, ln2_g, ln2_b,
           w_router, b_router, w_gate, w_up, w_down):
    depth = w_in.shape[0]
    alpha = (2.0 * depth) ** 0.25
    batch, seq, _ = x_prompt.shape
    n_dec, t_dec = x_sample.shape[:2]
    n_past = page_table.shape[1] * PAGE_SIZE
    assert t_dec == 1 and seq % (4 * CHUNK) == 0 and n_past % SEL_BLOCK == 0 and n_dec % SUBLANES == 0
    tm_p, tmx_p = 4 * CHUNK, 2 * CHUNK
    tm_s = tmx_s = n_dec
    wb = cache_win.shape[2]

    n_sub_p = seq // CMP_STRIDE
    ovt_p = jnp.asarray(np.pad(_overlap(n_sub_p - 1, seq // SEL_BLOCK), ((0, 1), (0, 0))).T, BF)
    n_sub_s = n_past // CMP_STRIDE
    n_sel_s = n_past // SEL_BLOCK + 1
    ov_s = jnp.asarray(np.pad(_overlap(n_sub_s - 1, n_sel_s), ((0, 1), (0, LANES - n_sel_s))), BF)
    expand = jnp.asarray(np.arange(LANES)[:, None] == (np.arange(n_past) // SEL_BLOCK)[None, :], BF)
    wrt = w_router.T.astype(F32)
    br = b_router[:, None].astype(F32)
    tri_p, tri_s = _strict_upper(tm_p), _strict_upper(tm_s)
    cache = cache_kv.reshape(cache_kv.shape[0], depth, PAGE_SIZE, N_KV_SLOTS * D_KV)
    cwin = cache_win.reshape(n_dec, depth, wb, 2 * D_KV)
    pt = page_table.reshape(-1).astype(jnp.int32)

    xp = x_prompt.reshape(batch * seq, D_MODEL)
    xs = x_sample.reshape(n_dec, D_MODEL)
    kv_p, win_p, kv_s, win_s, gv_s = [], [], [], [], []
    for l in range(depth):
        lw = _layer_weights(l, w_in, gmlp_ln_g, gmlp_ln_b, gmlp_ws, gmlp_bs, cmp_pos, cmp_w1, cmp_w2, w_branch_a,
                            w_branch_b, w_out, ln1_g, ln1_b, ln2_g, ln2_b, w_gate, w_up, w_down)
        (ya,) = _gmlp_call(xp, lw["wu"], lw["wv"], lw["wga"], lw["wpa"], lw["gln_g"], lw["gln_b"], lw["ws"],
                           lw["bs_t"], tm=tm_p, sample=False)
        sgb, kv, kvb, kvt, qt, gnt = _attn_in_prompt_call(xp, lw["wgb"], lw["wkv"], lw["wqt"], lw["wkvt"], lw["wgnt"],
                                                          tm=tm_p)
        kc, vct = _compress_call(kv, lw["cw"], batch=batch, seq=seq)
        ot = _nsa_prompt_call(qt, gnt, kvb, kvt, kc, vct, ovt_p, batch=batch, seq=seq)
        xr, cnt = _merge_call(xp, ya, sgb, ot, lw["wpb"], lw["wo"], lw["ln1_g"], lw["ln1_b"], wrt, br, tri_p,
                              tm=tm_p, alpha=alpha, o_transposed=True)
        xp = _moe(xr, cnt, lw, tm=tm_p, tmx=tmx_p, alpha=alpha)
        kvr = kv.reshape(batch, seq, 6, KV_HEADS, HEAD_DIM)
        kv_p.append(kvr[:, :, :N_KV_SLOTS])
        win_p.append(kvr[:, -min(WINDOW, seq):, N_KV_SLOTS:])
        ya, vn = _gmlp_call(xs, lw["wu"], lw["wv"], lw["wga"], lw["wpa"], lw["gln_g"], lw["gln_b"], lw["ws0"],
                            lw["bs0"], tm=tm_s, sample=True)
        sgb, kv, q, gn = _attn_in_sample_call(xs, lw["wgb"], lw["wkv"], lw["wq"], lw["wgn"])
        qh = q.reshape(n_dec, KV_HEADS, Q_PER_KV, HEAD_DIM)
        zq = jnp.zeros_like(qh[:, 0])
        qp = jnp.concatenate([jnp.concatenate([qh[:, 0], zq], -1), jnp.concatenate([zq, qh[:, 1]], -1)], 1).astype(BF)
        gates = jnp.pad(gn[:, :3 * N_HEADS].reshape(n_dec, N_HEADS, 3), ((0, 0), (0, 0), (0, LANES - 3)))
        o16 = _nsa_sample_call(pt, qp, kv[:, None, :], gates, cwin, cache, lw["cw"], ov_s, expand, layer=l,
                               n_past=n_past)
        o = jnp.concatenate([o16[:, :Q_PER_KV, :HEAD_DIM], o16[:, Q_PER_KV:, HEAD_DIM:]], 1).reshape(n_dec, D_ATTN)
        xr, cnt = _merge_call(xs, ya, sgb, o, lw["wpb"], lw["wo"], lw["ln1_g"], lw["ln1_b"], wrt, br, tri_s,
                              tm=tm_s, alpha=alpha, o_transposed=False)
        xs = _moe(xr, cnt, lw, tm=tm_s, tmx=tmx_s, alpha=alpha)
        kvr = kv.reshape(n_dec, 1, 6, KV_HEADS, HEAD_DIM)
        kv_s.append(kvr[:, :, :N_KV_SLOTS])
        win_s.append(jnp.concatenate([cache_win[:, l], kvr[:, :, N_KV_SLOTS:]], axis=1)[:, -wb:])
        gv_s.append(vn[:, None, :])
    return (xp.reshape(batch, seq, D_MODEL), xs.reshape(n_dec, 1, D_MODEL), jnp.stack(kv_p, axis=1),
            jnp.stack(win_p, axis=1), jnp.stack(kv_s, axis=1), jnp.stack(win_s, axis=1), jnp.stack(gv_s, axis=1))
```

```python
import functools

import numpy as np
import jax
import jax.numpy as jnp
from jax import lax
from jax.experimental import pallas as pl
from jax.experimental.pallas import tpu as pltpu

D_MODEL = 1024
CHUNK = 128
D_GMLP = 1024
GMLP_GROUPS = 8
N_HEADS = 16
KV_HEADS = 2
Q_PER_KV = N_HEADS // KV_HEADS
HEAD_DIM = 64
D_ATTN = N_HEADS * HEAD_DIM
D_KV = KV_HEADS * HEAD_DIM
CMP_BLOCK = 32
CMP_STRIDE = 16
CMP_HIDDEN = 2 * HEAD_DIM
SEL_BLOCK = 64
N_SELECT = 16
WINDOW = 512
Q_BLOCK = 128
FORCE_BONUS = 1e4
N_KV_SLOTS = 4
N_EXPERTS = 16
N_EXPERT_GROUPS = 4
EXPERTS_PER_GROUP = 4
D_EXPERT = 512
PAGE_SIZE = 128
PROJ_SIZES = (D_GMLP, D_GMLP, D_ATTN) + (D_KV,) * 6 + (3 * N_HEADS, D_MODEL, D_MODEL)

LANES = 128
SUBLANES = 8
VMEM_LIMIT_BYTES = 56 * 1024 * 1024

NEG = -1e30
LOG2E = 1.4426950408889634
KEY_TILE = 2 * Q_BLOCK
COL_CHUNK = 2 * LANES
CMP_VIS_ROWS = 128
BIAS_ROWS = 16
ONES_ROWS = 16
BF = jnp.bfloat16
F32 = jnp.float32
ROUTE_W = LANES
XR_W = D_MODEL + ROUTE_W
ROW_DMA_UNROLL = 8


def _cparams(sem):
    return pltpu.CompilerParams(dimension_semantics=sem, vmem_limit_bytes=VMEM_LIMIT_BYTES)


def _layer_norm(x, g, b, eps=1e-5):
    mu = jnp.mean(x, -1, keepdims=True)
    xc = x - mu
    var = jnp.mean(xc * xc, -1, keepdims=True)
    return xc * lax.rsqrt(var + eps) * g + b


def _dot(a, b):
    return jnp.dot(a, b, preferred_element_type=F32)


def _dot_nt(a, b):
    return lax.dot_general(a, b, (((1,), (1,)), ((), ())), preferred_element_type=F32)


def _dot_tn(a, b):
    return lax.dot_general(a, b, (((0,), (0,)), ((), ())), preferred_element_type=F32)


def _full(shape):
    nd = len(shape)
    return pl.BlockSpec(shape, lambda *_: (0,) * nd)


def _gmlp_kernel(x_ref, wu_ref, wv_ref, wga_ref, wpa_ref, lng_ref, lnb_ref, ws_ref, bs_ref,
                 ya_ref, *rest, sample):
    xb = x_ref[...].astype(BF)
    u = jax.nn.gelu(_dot(xb, wu_ref[...]))
    v = jax.nn.gelu(_dot(xb, wv_ref[...]))
    vn = _layer_norm(v, lng_ref[...], lnb_ref[...])
    tm = xb.shape[0]
    if sample:
        a = u * (vn * ws_ref[...] + bs_ref[...])
        rest[0][...] = vn
    else:
        n_chunk = tm // CHUNK
        vb = vn.astype(BF)
        row = lax.broadcasted_iota(jnp.int32, (CHUNK, CHUNK), 0)
        col = lax.broadcasted_iota(jnp.int32, (CHUNK, CHUNK), 1)
        causal = col <= row
        bs = bs_ref[...]
        cols = []
        for g in range(GMLP_GROUPS):
            lo = g * CHUNK
            wg = jnp.where(causal, ws_ref[g], 0.0).astype(BF)
            vg = jnp.concatenate([vb[c * CHUNK:(c + 1) * CHUNK, lo:lo + CHUNK] for c in range(n_chunk)], axis=1)
            mixed = _dot(wg, vg) + bs[:, g:g + 1]
            cols.append(jnp.concatenate([mixed[:, c * CHUNK:(c + 1) * CHUNK] for c in range(n_chunk)], axis=0))
        a = u * jnp.concatenate(cols, axis=1)
    ga = jax.nn.sigmoid(_dot(xb, wga_ref[...]))
    ya_ref[...] = ga * _dot(a.astype(BF), wpa_ref[...])


def _gmlp_call(x, wu, wv, wga, wpa, lng, lnb, ws, bs, *, tm, sample):
    n = x.shape[0]
    wspec = _full((D_MODEL, D_MODEL))
    vspec = _full((1, D_MODEL))
    in_specs = [pl.BlockSpec((tm, D_MODEL), lambda i: (i, 0)), wspec, wspec, wspec, wspec, vspec, vspec,
                _full(ws.shape), _full(bs.shape)]
    tok = pl.BlockSpec((tm, D_MODEL), lambda i: (i, 0))
    out_shape = [jax.ShapeDtypeStruct((n, D_MODEL), F32)]
    out_specs = [tok]
    if sample:
        out_shape.append(jax.ShapeDtypeStruct((n, D_MODEL), F32))
        out_specs.append(tok)
    return pl.pallas_call(
        functools.partial(_gmlp_kernel, sample=sample),
        grid=(n // tm,), in_specs=in_specs, out_specs=out_specs, out_shape=out_shape,
        compiler_params=_cparams(("arbitrary",)), name="gmlp_sample" if sample else "gmlp_prompt",
    )(x, wu, wv, wga, wpa, lng, lnb, ws, bs)


def _attn_in_prompt_kernel(x_ref, wgb_ref, wkv_ref, wqt_ref, wkvt_ref, wgnt_ref,
                           sgb_ref, kv_ref, kvb_ref, kvt_ref, qt_ref, gnt_ref):
    xb = x_ref[...].astype(BF)
    sgb_ref[...] = jax.nn.sigmoid(_dot(xb, wgb_ref[...]))
    kv = _dot(xb, wkv_ref[...])
    kv_ref[...] = kv
    kvb_ref[...] = kv.astype(BF)
    kvt_ref[...] = _dot_nt(wkvt_ref[...], xb).astype(BF)
    qt_ref[...] = (_dot_nt(wqt_ref[...], xb) * (LOG2E * HEAD_DIM ** -0.5)).astype(BF)
    gnt_ref[...] = jax.nn.sigmoid(_dot_nt(wgnt_ref[...], xb))


def _attn_in_prompt_call(x, wgb, wkv, wqt, wkvt, wgnt, *, tm):
    n = x.shape[0]
    nkv = wkv.shape[1]
    ngn = wgnt.shape[0]
    in_specs = [pl.BlockSpec((tm, D_MODEL), lambda i: (i, 0)), _full(wgb.shape), _full(wkv.shape),
                _full(wqt.shape), _full(wkvt.shape), _full(wgnt.shape)]
    out_shape = [jax.ShapeDtypeStruct((n, D_MODEL), F32), jax.ShapeDtypeStruct((n, nkv), F32),
                 jax.ShapeDtypeStruct((n, nkv), BF), jax.ShapeDtypeStruct((nkv, n), BF),
                 jax.ShapeDtypeStruct((D_ATTN, n), BF), jax.ShapeDtypeStruct((ngn, n), F32)]
    out_specs = [pl.BlockSpec((tm, D_MODEL), lambda i: (i, 0)), pl.BlockSpec((tm, nkv), lambda i: (i, 0)),
                 pl.BlockSpec((tm, nkv), lambda i: (i, 0)), pl.BlockSpec((nkv, tm), lambda i: (0, i)),
                 pl.BlockSpec((D_ATTN, tm), lambda i: (0, i)), pl.BlockSpec((ngn, tm), lambda i: (0, i))]
    return pl.pallas_call(
        _attn_in_prompt_kernel, grid=(n // tm,), in_specs=in_specs, out_specs=out_specs, out_shape=out_shape,
        compiler_params=_cparams(("arbitrary",)), name="attn_in_prompt",
    )(x, wgb, wkv, wqt, wkvt, wgnt)


def _attn_in_sample_kernel(x_ref, wgb_ref, wkv_ref, wq_ref, wgn_ref, sgb_ref, kv_ref, q_ref, gn_ref):
    xb = x_ref[...].astype(BF)
    sgb_ref[...] = jax.nn.sigmoid(_dot(xb, wgb_ref[...]))
    kv_ref[...] = _dot(xb, wkv_ref[...])
    q_ref[...] = _dot(xb, wq_ref[...]) * (HEAD_DIM ** -0.5)
    gn_ref[...] = jax.nn.sigmoid(_dot(xb, wgn_ref[...]))


def _attn_in_sample_call(x, wgb, wkv, wq, wgn):
    n = x.shape[0]
    nkv = wkv.shape[1]
    out_shape = [jax.ShapeDtypeStruct((n, D_MODEL), F32), jax.ShapeDtypeStruct((n, nkv), F32),
                 jax.ShapeDtypeStruct((n, D_ATTN), F32), jax.ShapeDtypeStruct((n, wgn.shape[1]), F32)]
    return pl.pallas_call(
        _attn_in_sample_kernel, grid=(1,),
        in_specs=[_full(x.shape), _full(wgb.shape), _full(wkv.shape), _full(wq.shape), _full(wgn.shape)],
        out_specs=[_full(s.shape) for s in out_shape], out_shape=out_shape,
        compiler_params=_cparams(("arbitrary",)), name="attn_in_sample",
    )(x, wgb, wkv, wq, wgn)


def _compress_rows(src_refs, n_sub, wp_ref, w2p_ref, w2pt_ref, ppos_ref):
    outs = []
    for slot in range(2):
        src_ref = src_refs[slot]
        x = jnp.concatenate(
            [src_ref[pl.ds(s, n_sub, stride=CMP_STRIDE), :].astype(BF) for s in range(CMP_STRIDE)],
            axis=1)
        hh = _dot(x, wp_ref[slot])
        pp = _dot(ppos_ref[slot], wp_ref[slot])
        pos = pp[0:1, 0:2 * CMP_HIDDEN] + pp[1:2, 2 * CMP_HIDDEN:]
        h1 = pltpu.roll(hh[:, 2 * CMP_HIDDEN:], n_sub - 1, 0)
        g = jax.nn.gelu(hh[:, 0:2 * CMP_HIDDEN] + h1 + pos).astype(BF)
        if slot == 0:
            outs.append(_dot(g, w2p_ref[0]))
        else:
            outs.append(_dot_nt(w2pt_ref[1], g))
    return outs


def _compress_kernel(kcm_ref, vcm_ref, wp_ref, w2p_ref, w2pt_ref, ppos_ref, kc_ref, vct_ref):
    n_sub = kc_ref.shape[0]
    kc, vct = _compress_rows((kcm_ref, vcm_ref), n_sub, wp_ref, w2p_ref, w2pt_ref, ppos_ref)
    kc_ref[...] = kc.astype(BF)
    vct_ref[...] = vct.astype(BF)


def _compress_call(kv, cw, *, batch, seq):
    n_sub = seq // CMP_STRIDE
    wp, w2p, w2pt, ppos = cw
    return pl.pallas_call(
        _compress_kernel, grid=(batch,),
        in_specs=[pl.BlockSpec((seq, LANES), lambda b: (b, 0)), pl.BlockSpec((seq, LANES), lambda b: (b, 1)),
                  _full(wp.shape), _full(w2p.shape), _full(w2pt.shape), _full(ppos.shape)],
        out_specs=[pl.BlockSpec((None, n_sub, LANES), lambda b: (b, 0, 0)),
                   pl.BlockSpec((None, LANES, n_sub), lambda b: (b, 0, 0))],
        out_shape=[jax.ShapeDtypeStruct((batch, n_sub, LANES), BF), jax.ShapeDtypeStruct((batch, LANES, n_sub), BF)],
        compiler_params=_cparams(("arbitrary",)), name="compress_prompt",
    )(kv, kv, wp, w2p, w2pt, ppos)


def _top_blocks(score):
    n_blk = score.shape[0]
    jf = lax.broadcasted_iota(jnp.int32, score.shape, 0).astype(F32)
    work = score
    sel = jnp.zeros_like(score)
    for _ in range(N_SELECT):
        mx = jnp.max(work, axis=0, keepdims=True)
        first = jnp.min(jnp.where(work == mx, jf, float(n_blk)), axis=0, keepdims=True)
        pick = jf == first
        sel = jnp.where(pick, 1.0, sel)
        work = jnp.where(pick, -3e38, work)
    return sel


def _nsa_prompt_kernel(qt_ref, gn_ref, ksel_ref, kwin_ref, vselt_ref, vwint_ref, kc_ref, vct_ref, ovt_ref, eblk_ref,
                       o_ref, sel_ref, bias_ref, m_ref, acc_ref, qtp_ref, tot_ref, sa_ref, sb_ref):
    i = pl.program_id(1)
    q0 = i * Q_BLOCK
    seq = ksel_ref.shape[0]
    n_cmp = kc_ref.shape[0]
    n_selb = ovt_ref.shape[0]
    nq = Q_PER_KV * Q_BLOCK
    qpos = q0 + lax.broadcasted_iota(jnp.int32, (1, Q_BLOCK), 1)
    blk =lax.broadcasted_iota(jnp.int32, (n_selb, Q_BLOCK), 0)
    cur = qpos // SEL_BLOCK
    forced = (blk == 0) | (blk == cur) | (blk == cur - 1)
    valid = blk * SEL_BLOCK <= qpos
    krow = lax.broadcasted_iota(jnp.int32, (KEY_TILE, Q_BLOCK), 0)
    tcol = lax.broadcasted_iota(jnp.int32, (KEY_TILE, Q_BLOCK), 1)
    zeros_q = jnp.zeros((HEAD_DIM, nq), BF)
    ones = jnp.ones((ONES_ROWS, KEY_TILE), BF)

    chunks = [(h, c) for h in range(KV_HEADS) for c in range(nq // COL_CHUNK)]

    def chunk_cols(c):
        return slice(c * COL_CHUNK, (c + 1) * COL_CHUNK)

    blocks_per_tile = KEY_TILE // SEL_BLOCK
    phantom_bias = n_selb // blocks_per_tile

    def set_bias(bias_idx):
        rows = [bias_ref[pl.ds(bias_idx * blocks_per_tile + jb, 1), :] for jb in range(blocks_per_tile)]
        tile = jnp.concatenate(rows + [jnp.zeros((BIAS_ROWS - blocks_per_tile, KV_HEADS * Q_BLOCK), F32)], axis=0)
        for h in range(KV_HEADS):
            qtp_ref[h, LANES:LANES + BIAS_ROWS, :] = jnp.concatenate(
                [tile[:, h * Q_BLOCK:(h + 1) * Q_BLOCK]] * Q_PER_KV, axis=1).astype(BF)

    def score_chunk(dst_ref, k_tile, h, c):
        k_aug = jnp.concatenate([k_tile, eblk_ref[...]], axis=1)
        dst_ref[h, :, chunk_cols(c)] = _dot(k_aug, qtp_ref[h, :, chunk_cols(c)])

    def flash_tile(src_ref, dst_ref, k_next, vt_tiles, masks, next_bias=None):
        vt_augs = [jnp.concatenate([vt, ones], axis=0) for vt in vt_tiles]
        if next_bias is not None:
            set_bias(next_bias)
        for h, c in chunks:
            cols = chunk_cols(c)
            if k_next is not None:
                score_chunk(dst_ref, k_next, h, c)
            m_old = m_ref[h, :, cols]
            ps, m_news = [], []
            for g in range(COL_CHUNK // Q_BLOCK):
                lo = c * COL_CHUNK + g * Q_BLOCK
                s = src_ref[h, :, lo:lo + Q_BLOCK]
                if masks is not None:
                    s = jnp.where(masks[h], s, NEG)
                m_new = jnp.maximum(m_old[:, g * Q_BLOCK:(g + 1) * Q_BLOCK], jnp.max(s, axis=0, keepdims=True))
                ps.append(jnp.exp2(s - m_new).astype(BF))
                m_news.append(m_new)
            m_new = jnp.concatenate(m_news, axis=1)
            acc_ref[h, :, cols] = (jnp.exp2(m_old - m_new) * acc_ref[h, :, cols]
                                   + _dot(vt_augs[h], jnp.concatenate(ps, axis=1)))
            m_ref[h, :, cols] = m_new

    def flash_branch(k_ref, vt_ref, n_tiles, tile_start, tile_masks, tile_bias=None):
        def k_tile(t):
            k0 = pl.multiple_of(jnp.clip(tile_start(t), 0, seq - KEY_TILE), Q_BLOCK)
            return k_ref[pl.ds(k0, KEY_TILE), :]

        def vt_tiles(t):
            k0 = pl.multiple_of(jnp.clip(tile_start(t), 0, seq - KEY_TILE), Q_BLOCK)
            return [head_rows(vt_ref, h)[:, pl.ds(k0, KEY_TILE)] for h in range(KV_HEADS)]

        m_ref[...] = jnp.full(m_ref.shape, NEG, F32)
        acc_ref[...] = jnp.zeros(acc_ref.shape, F32)
        n_loop = n_tiles if tile_bias is None else n_tiles - 1

        def first_scores(t, bias_idx):
            if bias_idx is not None:
                set_bias(bias_idx)
            first = k_tile(t)
            for h, c in chunks:
                score_chunk(sa_ref, first, h, c)

        def loop_masks(t):
            return tile_masks(t) if tile_bias is None else None

        def loop_bias(t):
            return None if tile_bias is None else jnp.where(t < n_loop, tile_bias(t), phantom_bias)

        first_scores(0, loop_bias(0))

        def body(j, carry):
            flash_tile(sa_ref, sb_ref, k_tile(2 * j + 1), vt_tiles(2 * j), loop_masks(2 * j), loop_bias(2 * j + 1))
            flash_tile(sb_ref, sa_ref, k_tile(2 * j + 2), vt_tiles(2 * j + 1), loop_masks(2 * j + 1), loop_bias(2 * j + 2))
            return carry

        lax.fori_loop(0, (n_loop + 1) // 2, body, 0)
        if tile_bias is not None:
            last = n_tiles - 1
            first_scores(last, tile_bias(last))
            flash_tile(sa_ref, None, None, vt_tiles(last), tile_masks(last))

    def head_rows(ref, h):
        return ref.at[h * HEAD_DIM:(h + 1) * HEAD_DIM]

    def flash_result(h):
        return acc_ref[h, 0:HEAD_DIM, :] * (1.0 / acc_ref[h, HEAD_DIM:HEAD_DIM + 1, :])

    def gate(h, c):
        r = (h * 3 + c) * Q_PER_KV
        return jnp.concatenate([gn_ref[r + g:r + g + 1, :] for g in range(Q_PER_KV)], axis=1)

    for h in range(KV_HEADS):
        qh = qt_ref[h * Q_PER_KV * HEAD_DIM:(h + 1) * Q_PER_KV * HEAD_DIM, :]
        qcat = jnp.concatenate([qh[g * HEAD_DIM:(g + 1) * HEAD_DIM, :] for g in range(Q_PER_KV)], axis=1)
        qtp = jnp.concatenate([qcat, zeros_q] if h == 0 else [zeros_q, qcat], axis=0)
        qtp_ref[h, 0:LANES, :] = qtp
        qtp_ref[h, LANES:2 * LANES, :] = jnp.zeros((LANES, nq), BF)

    def compressed_branch(n_rows):
        cmp_end = lax.broadcasted_iota(jnp.int32, (n_rows, Q_BLOCK), 0) * CMP_STRIDE + (CMP_BLOCK - 1)
        vis = cmp_end <= qpos
        any_vis = qpos >= CMP_BLOCK - 1
        for h in range(KV_HEADS):
            sc = _dot(kc_ref[0:n_rows, :], qtp_ref[h, 0:LANES, :])
            psum = jnp.zeros((n_rows, Q_BLOCK), F32)
            pcols = []
            for g in range(Q_PER_KV):
                s = jnp.where(vis, sc[:, g * Q_BLOCK:(g + 1) * Q_BLOCK], NEG)
                e = jnp.exp2(s - jnp.max(s, axis=0, keepdims=True))
                inv = jnp.where(any_vis, 1.0 / jnp.sum(e, axis=0, keepdims=True), 0.0)
                p = e * inv
                psum = psum + p
                pcols.append(p.astype(BF))
            tot_ref[h] = gate(h, 0) * _dot(vct_ref[h * HEAD_DIM:(h + 1) * HEAD_DIM, 0:n_rows],
                                           jnp.concatenate(pcols, axis=1))
            p_hi = psum.astype(BF)
            p_lo = (psum - p_hi.astype(F32)).astype(BF)
            imp = _dot(ovt_ref[:, 0:n_rows], p_hi) + _dot(ovt_ref[:, 0:n_rows], p_lo)
            sel_ref[:, h * Q_BLOCK:(h + 1) * Q_BLOCK] = jnp.where(valid, imp + jnp.where(forced, FORCE_BONUS, 0.0), NEG)

    vis_step = min(n_cmp, CMP_VIS_ROWS)
    last_vis = (q0 + Q_BLOCK - CMP_BLOCK) // CMP_STRIDE
    for c in range(n_cmp // vis_step):
        pl.when(last_vis // vis_step == c)(functools.partial(compressed_branch, (c + 1) * vis_step))

    picked = (_top_blocks(sel_ref[...]) > 0.5) & jnp.concatenate([valid] * KV_HEADS, axis=1)
    bias_ref[0:n_selb, :] = jnp.where(picked, 0.0, NEG)
    bias_ref[n_selb:n_selb + SUBLANES, :] = jnp.full((SUBLANES, KV_HEADS * Q_BLOCK), NEG, F32)

    def sel_masks(t):
        return [t * KEY_TILE + krow <= q0 + tcol] * KV_HEADS

    flash_branch(ksel_ref, vselt_ref, (q0 + Q_BLOCK + KEY_TILE - 1) // KEY_TILE, lambda t: t * KEY_TILE, sel_masks,
                 tile_bias=lambda t: t)
    for h in range(KV_HEADS):
        tot_ref[h] = tot_ref[h] + gate(h, 1) * flash_result(h)
        qtp_ref[h, LANES:LANES + BIAS_ROWS, :] = jnp.zeros((BIAS_ROWS, nq), BF)

    n_wt = (WINDOW + Q_BLOCK + KEY_TILE - 1) // KEY_TILE
    w0 = jnp.clip(q0 - WINDOW, 0, seq - n_wt * KEY_TILE)

    def win_masks(t):
        rel = (q0 + tcol) - (w0 + t * KEY_TILE + krow)
        return [(rel >= 0) & (rel < WINDOW)] * KV_HEADS

    flash_branch(kwin_ref, vwint_ref, n_wt, lambda t: w0 + t * KEY_TILE, win_masks)

    for h in range(KV_HEADS):
        tot = tot_ref[h] + gate(h, 2) * flash_result(h)
        for g in range(Q_PER_KV):
            r = (h * Q_PER_KV + g) * HEAD_DIM
            o_ref[r:r + HEAD_DIM, :] = tot[:, g * Q_BLOCK:(g + 1) * Q_BLOCK].astype(BF)


def _nsa_prompt_call(qt, gnt, kvb, kvt, kc, vct, ovt, *, batch, seq):
    nblk = seq // Q_BLOCK
    eblk = jnp.asarray(np.arange(LANES)[None, :] == (np.arange(KEY_TILE) // SEL_BLOCK)[:, None], BF)
    n = batch * seq
    n_sub = seq // CMP_STRIDE
    n_selb = seq // SEL_BLOCK
    nq = Q_PER_KV * Q_BLOCK
    in_specs = [
        pl.BlockSpec((D_ATTN, Q_BLOCK), lambda b, i: (0, b * nblk + i)),
        pl.BlockSpec((gnt.shape[0], Q_BLOCK), lambda b, i: (0, b * nblk + i)),
        pl.BlockSpec((seq, LANES), lambda b, i: (b, 2)),
        pl.BlockSpec((seq, LANES), lambda b, i: (b, 4)),
        pl.BlockSpec((LANES, seq), lambda b, i: (3, b)),
        pl.BlockSpec((LANES, seq), lambda b, i: (5, b)),
        pl.BlockSpec((None, n_sub, LANES), lambda b, i: (b, 0, 0)),
        pl.BlockSpec((None, LANES, n_sub), lambda b, i: (b, 0, 0)),
        _full(ovt.shape),
        _full(eblk.shape),
    ]
    return pl.pallas_call(
        _nsa_prompt_kernel, grid=(batch, nblk), in_specs=in_specs,
        out_specs=pl.BlockSpec((D_ATTN, Q_BLOCK), lambda b, i: (0, b * nblk + i)),
        out_shape=jax.ShapeDtypeStruct((D_ATTN, n), BF),
        scratch_shapes=[pltpu.VMEM((n_selb, KV_HEADS * Q_BLOCK), F32),
                        pltpu.VMEM((n_selb + SUBLANES, KV_HEADS * Q_BLOCK), F32), pltpu.VMEM((KV_HEADS, 1, nq), F32),
                        pltpu.VMEM((KV_HEADS, HEAD_DIM + ONES_ROWS, nq), F32),
                        pltpu.VMEM((KV_HEADS, 2 * LANES, nq), BF),
                        pltpu.VMEM((KV_HEADS, HEAD_DIM, nq), F32),
                        pltpu.VMEM((KV_HEADS, KEY_TILE, nq), F32), pltpu.VMEM((KV_HEADS, KEY_TILE, nq), F32)],
        compiler_params=_cparams(("arbitrary", "arbitrary")), name="nsa_prompt",
    )(qt, gnt, kvb, kvb, kvt, kvt, kc, vct, ovt, eblk)


def _route(x1, wrt_ref, br_ref, tri_ref, cnt_ref):
    tm = x1.shape[0]
    logits = lax.dot_general(wrt_ref[...], x1, (((1,), (1,)), ((), ())), precision=lax.Precision.HIGHEST,
                             preferred_element_type=F32)
    aff = jax.nn.sigmoid(logits)
    grp = aff + br_ref[...]
    affr = [aff[k:k + 1, :] for k in range(N_EXPERTS)]
    grpr = [grp[k:k + 1, :] for k in range(N_EXPERTS)]
    best = None
    gsel = jnp.zeros((1, tm), jnp.int32)
    for gi in range(N_EXPERT_GROUPS):
        m = grpr[gi * 4:(gi + 1) * 4]
        top2 = None
        for a in range(4):
            for b in range(a + 1, 4):
                s = m[a] + m[b]
                top2 = s if top2 is None else jnp.maximum(top2, s)
        if best is None:
            best = top2
        else:
            better = top2 > best
            gsel = jnp.where(better, gi, gsel)
            best = jnp.where(better, top2, best)
    ing, ina = [], []
    for j in range(4):
        vg, va = grpr[j], affr[j]
        for gi in range(1, N_EXPERT_GROUPS):
            vg = jnp.where(gsel == gi, grpr[gi * 4 + j], vg)
            va = jnp.where(gsel == gi, affr[gi * 4 + j], va)
        ing.append(vg)
        ina.append(va)

    def argmax4(vals):
        bv, bi = vals[0], jnp.zeros((1, tm), jnp.int32)
        for j in range(1, 4):
            better = vals[j] > bv
            bi = jnp.where(better, j, bi)
            bv = jnp.where(better, vals[j], bv)
        return bi

    loc1 = argmax4(ing)
    loc2 = argmax4([jnp.where(loc1 == j, -3e38, ing[j]) for j in range(4)])
    w1 = sum(jnp.where(loc1 == j, ina[j], 0.0) for j in range(4))
    w2 = sum(jnp.where(loc2 == j, ina[j], 0.0) for j in range(4))
    tot = w1 + w2
    wd = [jnp.where(loc1 == j, w1 / tot, 0.0) + jnp.where(loc2 == j, w2 / tot, 0.0) for j in range(4)]
    oh = jnp.concatenate([(gsel == gi).astype(F32) for gi in range(N_EXPERT_GROUPS)]
                         + [jnp.zeros((SUBLANES - N_EXPERT_GROUPS, tm), F32)], axis=0)
    cum = _dot(oh.astype(BF), tri_ref[...])
    carry = cnt_ref[...][:, 0:1]
    rank = jnp.sum(oh * (cum + carry), axis=0, keepdims=True)
    cnt_ref[...] = cnt_ref[...] + jnp.sum(oh, axis=1, keepdims=True)
    rows = wd + [gsel.astype(F32), rank, jnp.zeros((ROUTE_W - 6, tm), F32)]
    return jnp.concatenate(rows, axis=0)


def _merge_kernel(x_ref, ya_ref, sgb_ref, o_ref, wpb_ref, wo_ref, g_ref, b_ref, wrt_ref, br_ref, tri_ref,
                  xr_ref, cnt_out_ref, cnt_ref, *, alpha, o_transposed):
    @pl.when(pl.program_id(0) == 0)
    def _():
        cnt_ref[...] = jnp.zeros(cnt_ref.shape, F32)

    if o_transposed:
        ob = _dot_tn(o_ref[...], wpb_ref[...])
    else:
        ob = _dot(o_ref[...].astype(BF), wpb_ref[...])
    y = ya_ref[...] + sgb_ref[...] * ob
    mix = _dot(y.astype(BF), wo_ref[...])
    x1 = _layer_norm(alpha * x_ref[...] + mix, g_ref[...], b_ref[...])
    xr_ref[:, 0:D_MODEL] = x1
    info = _route(x1, wrt_ref, br_ref, tri_ref, cnt_ref)
    xr_ref[:, D_MODEL:XR_W] = info.T
    cnt_out_ref[...] = cnt_ref[...]


def _merge_call(x, ya, sgb, o, wpb, wo, g, b, wrt, br, tri, *, tm, alpha, o_transposed):
    n = ya.shape[0]
    tok = pl.BlockSpec((tm, D_MODEL), lambda i: (i, 0))
    ospec = pl.BlockSpec((D_ATTN, tm), lambda i: (0, i)) if o_transposed else tok
    return pl.pallas_call(
        functools.partial(_merge_kernel, alpha=alpha, o_transposed=o_transposed), grid=(n // tm,),
        in_specs=[tok, tok, tok, ospec, _full(wpb.shape), _full(wo.shape), _full(g.shape), _full(b.shape),
                  _full(wrt.shape), _full(br.shape), _full(tri.shape)],
        out_specs=[pl.BlockSpec((tm, XR_W), lambda i: (i, 0)), _full((SUBLANES, LANES))],
        out_shape=[jax.ShapeDtypeStruct((n, XR_W), F32), jax.ShapeDtypeStruct((SUBLANES, LANES), F32)],
        scratch_shapes=[pltpu.VMEM((SUBLANES, LANES), F32)],
        compiler_params=_cparams(("arbitrary",)), name="merge_t" if o_transposed else "merge_n",
    )(x, ya, sgb, o, wpb, wo, g, b, wrt, br, tri)


def _row_copy(src_ref, dst_ref, sem, src_row, dst_row):
    return pltpu.make_async_copy(src_ref.at[pl.ds(src_row, 1), :], dst_ref.at[pl.ds(dst_row, 1), :], sem)


def _scatter_kernel(dest_ref, xr_ref, xs_in_ref, xs_ref, sem):
    del xs_in_ref
    tm = xr_ref.shape[0]

    def start(r, c):
        _row_copy(xr_ref, xs_ref, sem, r, dest_ref[r]).start()
        return c

    def wait(r, c):
        _row_copy(xr_ref, xs_ref, sem, r, dest_ref[r]).wait()
        return c

    lax.fori_loop(0, tm, start, 0, unroll=ROW_DMA_UNROLL)
    lax.fori_loop(0, tm, wait, 0, unroll=ROW_DMA_UNROLL)


def _scatter_call(dest, xr, xs0, *, tm):
    n = xr.shape[0]
    return pl.pallas_call(
        _scatter_kernel, grid=(n // tm,),
        in_specs=[pl.BlockSpec((tm,), lambda i: (i,), memory_space=pltpu.SMEM),
                  pl.BlockSpec((tm, XR_W), lambda i: (i, 0)), pl.BlockSpec(memory_space=pl.ANY)],
        out_specs=pl.BlockSpec(memory_space=pl.ANY),
        out_shape=jax.ShapeDtypeStruct(xs0.shape, F32),
        scratch_shapes=[pltpu.SemaphoreType.DMA(())],
        input_output_aliases={2: 0},
        compiler_params=_cparams(("arbitrary",)), name="moe_scatter",
    )(dest, xr, xs0)


def _experts_kernel(tg_ref, nu_ref, xs_ref, wg_ref, wu_ref, wd_ref, ys_ref):
    del tg_ref
    j = pl.program_id(0)

    @pl.when(j < nu_ref[0])
    def _():
        xb = xs_ref[:, 0:D_MODEL].astype(BF)
        acc = None
        for e in range(EXPERTS_PER_GROUP):
            hidden = jax.nn.silu(_dot(xb, wg_ref[e])) * _dot(xb, wu_ref[e])
            hidden = hidden * xs_ref[:, D_MODEL + e:D_MODEL + e + 1]
            y = _dot(hidden.astype(BF), wd_ref[e])
            acc = y if acc is None else acc + y
        ys_ref[...] = acc

    @pl.when(j >= nu_ref[0])
    def _():
        ys_ref[...] = jnp.zeros(ys_ref.shape, F32)


def _experts_call(tile_group, n_used, xs, wg, wu, wd, *, tmx):
    rows = xs.shape[0]
    wspec_in = pl.BlockSpec((None, EXPERTS_PER_GROUP, D_MODEL, D_EXPERT), lambda j, tg, nu: (tg[j], 0, 0, 0))
    wspec_out = pl.BlockSpec((None, EXPERTS_PER_GROUP, D_EXPERT, D_MODEL), lambda j, tg, nu: (tg[j], 0, 0, 0))
    grid_spec = pltpu.PrefetchScalarGridSpec(
        num_scalar_prefetch=2, grid=(rows // tmx,),
        in_specs=[pl.BlockSpec((tmx, XR_W), lambda j, tg, nu: (j, 0)), wspec_in, wspec_in, wspec_out],
        out_specs=pl.BlockSpec((tmx, D_MODEL), lambda j, tg, nu: (j, 0)))
    return pl.pallas_call(
        _experts_kernel, grid_spec=grid_spec, out_shape=jax.ShapeDtypeStruct((rows, D_MODEL), F32),
        compiler_params=_cparams(("arbitrary",)), name="moe_experts",
    )(tile_group, n_used, xs, wg, wu, wd)


def _combine_kernel(dest_ref, xr_ref, ys_ref, g_ref, b_ref, x2_ref, buf_ref, sem, *, alpha):
    tm = xr_ref.shape[0]

    def start(r, c):
        _row_copy(ys_ref, buf_ref, sem, dest_ref[r], r).start()
        return c

    def wait(r, c):
        _row_copy(ys_ref, buf_ref, sem, dest_ref[r], r).wait()
        return c

    lax.fori_loop(0, tm, start, 0, unroll=ROW_DMA_UNROLL)
    lax.fori_loop(0, tm, wait, 0, unroll=ROW_DMA_UNROLL)
    x2_ref[...] = _layer_norm(alpha * xr_ref[...] + buf_ref[...], g_ref[...], b_ref[...])


def _combine_call(dest, xr, ys, g, b, *, tm, alpha):
    n = xr.shape[0]
    return pl.pallas_call(
        functools.partial(_combine_kernel, alpha=alpha), grid=(n // tm,),
        in_specs=[pl.BlockSpec((tm,), lambda i: (i,), memory_space=pltpu.SMEM),
                  pl.BlockSpec((tm, D_MODEL), lambda i: (i, 0)), pl.BlockSpec(memory_space=pl.ANY),
                  _full(g.shape), _full(b.shape)],
        out_specs=pl.BlockSpec((tm, D_MODEL), lambda i: (i, 0)),
        out_shape=jax.ShapeDtypeStruct((n, D_MODEL), F32),
        scratch_shapes=[pltpu.VMEM((tm, D_MODEL), F32), pltpu.SemaphoreType.DMA(())],
        compiler_params=_cparams(("arbitrary",)), name="moe_combine",
    )(dest, xr, ys, g, b)


def _moe(xr, cnt, lw, *, tm, tmx, alpha):
    n = xr.shape[0]
    n_tiles = n // tmx + N_EXPERT_GROUPS
    counts = cnt[:N_EXPERT_GROUPS, 0].astype(jnp.int32)
    tiles_per = (counts + tmx - 1) // tmx
    tile_end = jnp.cumsum(tiles_per)
    offs = (tile_end - tiles_per) * tmx
    gid = xr[:, D_MODEL + 4].astype(jnp.int32)
    rank = xr[:, D_MODEL + 5].astype(jnp.int32)
    dest = offs[gid] + rank
    tile_group = jnp.minimum(jnp.sum(jnp.arange(n_tiles)[:, None] >= tile_end[None, :], axis=1),
                             N_EXPERT_GROUPS - 1).astype(jnp.int32)
    n_used = tile_end[-1:].astype(jnp.int32)
    xs = _scatter_call(dest, xr, jnp.zeros((n_tiles * tmx, XR_W), F32), tm=tm)
    ys = _experts_call(tile_group, n_used, xs, lw["wg"], lw["wu_e"], lw["wd"], tmx=tmx)
    return _combine_call(dest, xr, ys, lw["ln2_g"], lw["ln2_b"], tm=tm, alpha=alpha)


def _softmax_with_new(s, mask, s_new, new_on):
    s = jnp.where(mask, s, NEG)
    s_new = jnp.where(new_on, s_new, NEG)
    mx = jnp.maximum(jnp.max(s, axis=1, keepdims=True), s_new)
    p = jnp.where(mask, jnp.exp(s - mx), 0.0)
    p_new = jnp.where(new_on, jnp.exp(s_new - mx), 0.0)
    den = jnp.maximum(jnp.sum(p, axis=1, keepdims=True) + p_new, 1e-30)
    return p, p_new, 1.0 / den


def _sample_attention(qs, news, gates, cw_refs, x_refs, sel_refs, wph_ref, w2p_ref, w2pt_ref, pposh_ref, ov_ref,
                      exp_ref, *, n_past):
    n_seq = len(qs)
    seqs = range(n_seq)
    n_sub = n_past // CMP_STRIDE
    n_cmp = n_sub - CMP_BLOCK // CMP_STRIDE + 1
    nrow = KV_HEADS * Q_PER_KV
    qpos = n_past
    qfs = [q.astype(F32) for q in qs]

    def new_score(k, col):
        kn = news[k][:, col * LANES:(col + 1) * LANES].astype(BF).astype(F32)
        return jnp.sum(qfs[k] * kn, axis=1, keepdims=True)

    def new_value(k, col):
        return news[k][:, col * LANES:(col + 1) * LANES].astype(BF).astype(F32)

    def rows(x, idx, n):
        return x[idx * n:(idx + 1) * n]

    g_all = []
    for slot in range(2):
        w = wph_ref[slot]
        pp = _dot(pposh_ref[slot], w)
        pos = pp[0:1, 0:CMP_HIDDEN] + pp[1:2, CMP_HIDDEN:]
        x_all = jnp.concatenate([x_refs[k][slot * KV_HEADS + h].astype(BF) for k in seqs for h in range(KV_HEADS)], axis=0)
        hh_all = _dot(x_all, w)
        g_seq = []
        for k in seqs:
            g_heads = []
            for h in range(KV_HEADS):
                hh = rows(hh_all, k * KV_HEADS + h, n_sub)
                h1 = pltpu.roll(hh[:, CMP_HIDDEN:], n_sub - 1, 0)
                g_heads.append(jax.nn.gelu(hh[:, 0:CMP_HIDDEN] + h1 + pos).astype(BF))
            g_seq.append(jnp.concatenate(g_heads, axis=1))
        g_all.append(jnp.concatenate(g_seq, axis=0))
    kc_all = _dot(g_all[0], w2p_ref[0]).astype(BF)
    vct_all = _dot_nt(w2pt_ref[1], g_all[1]).astype(BF)

    n_idx = lax.broadcasted_iota(jnp.int32, (nrow, n_sub), 1)
    vis = (n_idx * CMP_STRIDE + (CMP_BLOCK - 1) <= qpos) & (n_idx < n_cmp)
    s_cs = [_dot_nt(qs[k], rows(kc_all, k, n_sub)) for k in seqs]
    p_cs = []
    for k in seqs:
        s_c = jnp.where(vis, s_cs[k], NEG)
        mx = jnp.max(s_c, axis=1, keepdims=True)
        p_c = jnp.where(vis, jnp.exp(s_c - mx), 0.0)
        p_cs.append(p_c * (1.0 / jnp.maximum(jnp.sum(p_c, axis=1, keepdims=True), 1e-30)))
    o_cs = [_dot_nt(p_cs[k].astype(BF), vct_all[:, k * n_sub:(k + 1) * n_sub]) for k in seqs]

    psum = jnp.concatenate([jnp.sum(p_cs[k][h * Q_PER_KV:(h + 1) * Q_PER_KV], axis=0, keepdims=True)
                            for k in seqs for h in range(KV_HEADS)], axis=0)
    imp_rows = -(-psum.shape[0] // SUBLANES) * SUBLANES
    if imp_rows > psum.shape[0]:
        psum = jnp.concatenate([psum, jnp.zeros((imp_rows - psum.shape[0], n_sub), F32)], axis=0)
    p_hi = psum.astype(BF)
    p_lo = (psum - p_hi.astype(F32)).astype(BF)
    imp = _dot(p_hi, ov_ref[...]) + _dot(p_lo, ov_ref[...])
    n_selp = imp.shape[1]
    j_idx = lax.broadcasted_iota(jnp.int32, (imp_rows, n_selp), 1)
    cur = qpos // SEL_BLOCK
    forced = (j_idx == 0) | (j_idx == cur) | (j_idx == cur - 1)
    valid = j_idx * SEL_BLOCK <= qpos
    score = jnp.where(valid, imp + jnp.where(forced, FORCE_BONUS, 0.0), NEG)
    score_t = jnp.concatenate([score, jnp.zeros((n_selp - imp_rows, n_selp), F32)], axis=0).T
    jr = lax.broadcasted_iota(jnp.int32, (n_selp, n_selp), 0)
    jc = lax.broadcasted_iota(jnp.int32, (n_selp, n_selp), 1)
    sel_rows = []
    for r in range(n_seq * KV_HEADS):
        other = jnp.broadcast_to(score_t[:, r:r + 1], (n_selp, n_selp))
        mine = jnp.broadcast_to(score[r:r + 1, :], (n_selp, n_selp))
        ahead = (other > mine) | ((other == mine) & (jr < jc))
        rank = jnp.sum(ahead.astype(F32), axis=0, keepdims=True)
        sel_rows.append(jnp.broadcast_to((rank < N_SELECT).astype(F32), (Q_PER_KV, n_selp)))
    sel = jnp.concatenate(sel_rows, axis=0)

    on_all = _dot(sel.astype(BF), exp_ref[...]) > 0.5
    s_ss = [_dot_nt(qs[k], sel_refs[k][:, 0:LANES]) for k in seqs]
    soft = [_softmax_with_new(s_ss[k], rows(on_all, k, nrow), new_score(k, 2), rows(sel, k, nrow)[:, cur:cur + 1] > 0.5)
            for k in seqs]
    pv_s = [_dot(soft[k][0].astype(BF), sel_refs[k][:, LANES:2 * LANES]) for k in seqs]
    o_ss = [(pv_s[k] + soft[k][1] * new_value(k, 3)) * soft[k][2] for k in seqs]

    wb = cw_refs[0].shape[0]
    wpos = (n_past - wb) + lax.broadcasted_iota(jnp.int32, (nrow, wb), 1)
    rel = qpos - wpos
    wmask = (rel >= 0) & (rel < WINDOW) & (wpos >= 0)
    s_ws = [_dot_nt(qs[k], cw_refs[k][:, 0:LANES]) for k in seqs]
    soft = [_softmax_with_new(s_ws[k], wmask, new_score(k, 4), jnp.full((nrow, 1), True)) for k in seqs]
    pv_w = [_dot(soft[k][0].astype(BF), cw_refs[k][:, LANES:2 * LANES]) for k in seqs]
    o_ws = [(pv_w[k] + soft[k][1] * new_value(k, 5)) * soft[k][2] for k in seqs]

    return [gates[k][:, 0:1] * o_cs[k] + gates[k][:, 1:2] * o_ss[k] + gates[k][:, 2:3] * o_ws[k] for k in seqs]


SEQ_PER_STEP = 4


def _nsa_sample_kernel(pt_ref, q_ref, new_ref, gate_ref, cw_ref, cmp_ref, selc_ref, wph_ref, w2p_ref, w2pt_ref,
                       pposh_ref, ov_ref, exp_ref, o_ref, x_buf, sel_buf, sems, *, layer, n_pages, n_past):
    j = pl.program_id(0)
    n_steps = pl.num_programs(0)
    rows_per_page = PAGE_SIZE // CMP_STRIDE
    cur = lax.rem(j, 2)

    def step_copies(buf_set, step):
        copies = []
        for k in range(SEQ_PER_STEP):
            for p in range(n_pages):
                page = pt_ref[(step * SEQ_PER_STEP + k) * n_pages + p]
                copies.append(pltpu.make_async_copy(
                    cmp_ref.at[page, layer], x_buf.at[buf_set, k, :, pl.ds(p * rows_per_page, rows_per_page), :],
                    sems.at[buf_set]))
                copies.append(pltpu.make_async_copy(
                    selc_ref.at[page, layer], sel_buf.at[buf_set, k, pl.ds(p * PAGE_SIZE, PAGE_SIZE), :],
                    sems.at[buf_set]))
        return copies

    @pl.when(j == 0)
    def _():
        for c in step_copies(cur, j):
            c.start()

    @pl.when(j + 1 < n_steps)
    def _():
        for c in step_copies(1 - cur, j + 1):
            c.start()

    for c in step_copies(cur, j):
        c.wait()
    ks = range(SEQ_PER_STEP)
    outs = _sample_attention([q_ref[k] for k in ks], [new_ref[k] for k in ks], [gate_ref[k] for k in ks],
                             [cw_ref.at[k] for k in ks], [x_buf.at[cur, k] for k in ks],
                             [sel_buf.at[cur, k] for k in ks], wph_ref, w2p_ref, w2pt_ref, pposh_ref, ov_ref, exp_ref,
                             n_past=n_past)
    for k in ks:
        o_ref[k] = outs[k]


def _nsa_sample_call(pt, qp, new, gates, cwin, cache_cmp, cache_sel, cw_s, ov, expand, *, layer, n_past):
    n_dec = qp.shape[0]
    n_pages = n_past // PAGE_SIZE
    n_sub = n_past // CMP_STRIDE
    nrow = KV_HEADS * Q_PER_KV
    wb = cwin.shape[2]
    wph, w2p, w2pt, pposh = cw_s
    const = lambda shape: pl.BlockSpec(shape, lambda b, pt_: (0,) * len(shape))
    per_seq = lambda *tail: pl.BlockSpec((SEQ_PER_STEP,) + tail, lambda b, pt_: (b,) + (0,) * len(tail))
    grid_spec = pltpu.PrefetchScalarGridSpec(
        num_scalar_prefetch=1, grid=(n_dec // SEQ_PER_STEP,),
        in_specs=[per_seq(nrow, LANES), per_seq(1, new.shape[2]), per_seq(nrow, LANES),
                  pl.BlockSpec((SEQ_PER_STEP, None, wb, 2 * LANES), lambda b, pt_: (b, layer, 0, 0)),
                  pl.BlockSpec(memory_space=pl.ANY), pl.BlockSpec(memory_space=pl.ANY),
                  const(wph.shape), const(w2p.shape), const(w2pt.shape), const(pposh.shape), const(ov.shape),
                  const(expand.shape)],
        out_specs=per_seq(nrow, LANES),
        scratch_shapes=[pltpu.VMEM((2, SEQ_PER_STEP, 2 * KV_HEADS, n_sub, CMP_STRIDE * HEAD_DIM), F32),
                        pltpu.VMEM((2, SEQ_PER_STEP, n_past, 2 * LANES), BF),
                        pltpu.SemaphoreType.DMA((2,))])
    return pl.pallas_call(
        functools.partial(_nsa_sample_kernel, layer=layer, n_pages=n_pages, n_past=n_past),
        grid_spec=grid_spec, out_shape=jax.ShapeDtypeStruct((n_dec, nrow, LANES), F32),
        compiler_params=_cparams(("arbitrary",)), name="nsa_sample",
    )(pt, qp, new, gates, cwin, cache_cmp, cache_sel, wph, w2p, w2pt, pposh, ov, expand)


def _overlap(n_cmp, n_sel):
    c0 = np.arange(n_cmp) * CMP_STRIDE
    s0 = np.arange(n_sel) * SEL_BLOCK
    return ((c0[:, None] < s0[None, :] + SEL_BLOCK) & (c0[:, None] + CMP_BLOCK > s0[None, :])).astype(np.float32)


def _layer_weights(l, w_in, gmlp_ln_g, gmlp_ln_b, gmlp_ws, gmlp_bs, cmp_pos, cmp_w1, cmp_w2, w_branch_a, w_branch_b,
                   w_out, ln1_g, ln1_b, ln2_g, ln2_b, w_gate, w_up, w_down):
    cuts = np.cumsum((0,) + PROJ_SIZES)
    seg = lambda i, j=None: w_in[l][:, cuts[i]:cuts[i + 1 if j is None else j]].astype(BF)
    wgn = seg(9)
    eye = jnp.eye(KV_HEADS, dtype=F32)
    r = CMP_BLOCK // CMP_STRIDE
    w1r = cmp_w1[l].reshape(2, r, CMP_STRIDE, HEAD_DIM, CMP_HIDDEN)
    wp = jnp.einsum("xrsdf,hk->xshdrkf", w1r, eye).reshape(2, CMP_STRIDE * D_KV, r * KV_HEADS * CMP_HIDDEN)
    w2p = jnp.einsum("xfd,hk->xhfkd", cmp_w2[l], eye).reshape(2, KV_HEADS * CMP_HIDDEN, D_KV)
    pos = cmp_pos[l].reshape(2, r, CMP_STRIDE, 1, HEAD_DIM)
    ppos = jnp.broadcast_to(pos, (2, r, CMP_STRIDE, KV_HEADS, HEAD_DIM)).reshape(2, r, CMP_STRIDE * D_KV)
    ppos = jnp.pad(ppos, ((0, 0), (0, SUBLANES - r), (0, 0)))
    ex = lambda w: w.reshape((N_EXPERT_GROUPS, EXPERTS_PER_GROUP) + w.shape[1:]).astype(BF)
    row = lambda v: v[l][None, :].astype(F32)
    return {
        "wu": seg(0), "wv": seg(1), "wq": seg(2), "wkv": seg(3, 9), "wga": seg(10), "wgb": seg(11),
        "wqt": seg(2).T, "wkvt": seg(3, 9).T,
        "wgnt": wgn.reshape(D_MODEL, KV_HEADS, Q_PER_KV, 3).transpose(1, 3, 2, 0).reshape(3 * N_HEADS, D_MODEL),
        "wgn": jnp.pad(wgn, ((0, 0), (0, LANES - 3 * N_HEADS))),
        "gln_g": row(gmlp_ln_g), "gln_b": row(gmlp_ln_b),
        "ws": gmlp_ws[l], "bs_t": gmlp_bs[l].T,
        "ws0": jnp.repeat(gmlp_ws[l][:, 0, 0], CHUNK)[None, :], "bs0": jnp.repeat(gmlp_bs[l][:, 0], CHUNK)[None, :],
        "cw": (wp.astype(BF), w2p.astype(BF), w2p.transpose(0, 2, 1).astype(BF), ppos.astype(BF)),
        "cw_s": (w1r.transpose(0, 2, 3, 1, 4).reshape(2, CMP_STRIDE * HEAD_DIM, r * CMP_HIDDEN).astype(BF),
                 w2p.astype(BF), w2p.transpose(0, 2, 1).astype(BF),
                 jnp.pad(cmp_pos[l].reshape(2, r, CMP_STRIDE * HEAD_DIM), ((0, 0), (0, SUBLANES - r), (0, 0))).astype(BF)),
        "wpa": w_branch_a[l].astype(BF), "wpb": w_branch_b[l].astype(BF), "wo": w_out[l].astype(BF),
        "ln1_g": row(ln1_g), "ln1_b": row(ln1_b), "ln2_g": row(ln2_g), "ln2_b": row(ln2_b),
        "wg": ex(w_gate[l]), "wu_e": ex(w_up[l]), "wd": ex(w_down[l]),
    }


def _strict_upper(n):
    return jnp.asarray(np.triu(np.ones((n, n), np.float32), 1), BF)


def kernel(x_prompt, x_sample, cache_kv, cache_win, page_table, w_in, gmlp_ln_g, gmlp_ln_b, gmlp_ws, gmlp_bs,
           cmp_pos, cmp_w1, cmp_w2, w_branch_a, w_branch_b, w_out, ln1_g, ln1_b, ln2_g, ln2_b,
           w_router, b_router, w_gate, w_up, w_down):
    depth = w_in.shape[0]
    alpha = (2.0 * depth) ** 0.25
    batch, seq, _ = x_prompt.shape
    n_dec, t_dec = x_sample.shape[:2]
    n_past = page_table.shape[1] * PAGE_SIZE
    assert t_dec == 1 and seq % (4 * CHUNK) == 0 and n_past % PAGE_SIZE == 0 and n_dec % SUBLANES == 0
    tm_p, tmx_p = 4 * CHUNK, 2 * CHUNK
    tm_s = tmx_s = n_dec
    wb = cache_win.shape[2]

    n_sub_p = seq // CMP_STRIDE
    ovt_p = jnp.asarray(np.pad(_overlap(n_sub_p - 1, seq // SEL_BLOCK), ((0, 1), (0, 0))).T, BF)
    n_sub_s = n_past // CMP_STRIDE
    n_sel_s = n_past // SEL_BLOCK + 1
    ov_s = jnp.asarray(np.pad(_overlap(n_sub_s - 1, n_sel_s), ((0, 1), (0, LANES - n_sel_s))), BF)
    expand = jnp.asarray(np.arange(LANES)[:, None] == (np.arange(n_past) // SEL_BLOCK)[None, :], BF)
    wrt = w_router.T.astype(F32)
    br = b_router[:, None].astype(F32)
    tri_p, tri_s = _strict_upper(tm_p), _strict_upper(tm_s)
    n_phys = cache_kv.shape[0]
    cache_cmp = cache_kv[:, :, :, 0:2].reshape(
        n_phys, depth, PAGE_SIZE // CMP_STRIDE, CMP_STRIDE, 2, KV_HEADS, HEAD_DIM).transpose(0, 1, 4, 5, 2, 3, 6).reshape(
        n_phys, depth, 2 * KV_HEADS, PAGE_SIZE // CMP_STRIDE, CMP_STRIDE * HEAD_DIM)
    cache_sel = cache_kv[:, :, :, 2:4].reshape(n_phys, depth, PAGE_SIZE, 2 * D_KV).astype(BF)
    cwin = cache_win.reshape(n_dec, depth, wb, 2 * D_KV).astype(BF)
    pt = page_table.reshape(-1).astype(jnp.int32)

    xp = x_prompt.reshape(batch * seq, D_MODEL)
    xs = x_sample.reshape(n_dec, D_MODEL)
    kv_p, win_p, kv_s, win_s, gv_s = [], [], [], [], []
    for l in range(depth):
        lw = _layer_weights(l, w_in, gmlp_ln_g, gmlp_ln_b, gmlp_ws, gmlp_bs, cmp_pos, cmp_w1, cmp_w2, w_branch_a,
                            w_branch_b, w_out, ln1_g, ln1_b, ln2_g, ln2_b, w_gate, w_up, w_down)
        (ya,) = _gmlp_call(xp, lw["wu"], lw["wv"], lw["wga"], lw["wpa"], lw["gln_g"], lw["gln_b"], lw["ws"],
                           lw["bs_t"], tm=tm_p, sample=False)
        sgb, kv, kvb, kvt, qt, gnt = _attn_in_prompt_call(xp, lw["wgb"], lw["wkv"], lw["wqt"], lw["wkvt"], lw["wgnt"],
                                                          tm=tm_p)
        kc, vct = _compress_call(kv, lw["cw"], batch=batch, seq=seq)
        ot = _nsa_prompt_call(qt, gnt, kvb, kvt, kc, vct, ovt_p, batch=batch, seq=seq)
        xr, cnt = _merge_call(xp, ya, sgb, ot, lw["wpb"], lw["wo"], lw["ln1_g"], lw["ln1_b"], wrt, br, tri_p,
                              tm=tm_p, alpha=alpha, o_transposed=True)
        xp = _moe(xr, cnt, lw, tm=tm_p, tmx=tmx_p, alpha=alpha)
        kvr = kv.reshape(batch, seq, 6, KV_HEADS, HEAD_DIM)
        kv_p.append(kvr[:, :, :N_KV_SLOTS])
        win_p.append(kvr[:, -min(WINDOW, seq):, N_KV_SLOTS:])
        ya, vn = _gmlp_call(xs, lw["wu"], lw["wv"], lw["wga"], lw["wpa"], lw["gln_g"], lw["gln_b"], lw["ws0"],
                            lw["bs0"], tm=tm_s, sample=True)
        sgb, kv, q, gn = _attn_in_sample_call(xs, lw["wgb"], lw["wkv"], lw["wq"], lw["wgn"])
        qh = q.reshape(n_dec, KV_HEADS, Q_PER_KV, HEAD_DIM)
        zq = jnp.zeros_like(qh[:, 0])
        qp = jnp.concatenate([jnp.concatenate([qh[:, 0], zq], -1), jnp.concatenate([zq, qh[:, 1]], -1)], 1).astype(BF)
        gates = jnp.pad(gn[:, :3 * N_HEADS].reshape(n_dec, N_HEADS, 3), ((0, 0), (0, 0), (0, LANES - 3)))
        o16 = _nsa_sample_call(pt, qp, kv[:, None, :], gates, cwin, cache_cmp, cache_sel, lw["cw_s"], ov_s, expand,
                               layer=l, n_past=n_past)
        o = jnp.concatenate([o16[:, :Q_PER_KV, :HEAD_DIM], o16[:, Q_PER_KV:, HEAD_DIM:]], 1).reshape(n_dec, D_ATTN)
        xr, cnt = _merge_call(xs, ya, sgb, o, lw["wpb"], lw["wo"], lw["ln1_g"], lw["ln1_b"], wrt, br, tri_s,
                              tm=tm_s, alpha=alpha, o_transposed=False)
        xs = _moe(xr, cnt, lw, tm=tm_s, tmx=tmx_s, alpha=alpha)
        kvr = kv.reshape(n_dec, 1, 6, KV_HEADS, HEAD_DIM)
        kv_s.append(kvr[:, :, :N_KV_SLOTS])
        win_s.append(jnp.concatenate([cache_win[:, l], kvr[:, :, N_KV_SLOTS:]], axis=1)[:, -wb:])
        gv_s.append(vn[:, None, :])
    return (xp.reshape(batch, seq, D_MODEL), xs.reshape(n_dec, 1, D_MODEL), jnp.stack(kv_p, axis=1),
            jnp.stack(win_p, axis=1), jnp.stack(kv_s, axis=1), jnp.stack(win_s, axis=1), jnp.stack(gv_s, axis=1))
```

```python
import functools

import numpy as np
import jax
import jax.numpy as jnp
from jax import lax
from jax.experimental import pallas as pl
from jax.experimental.pallas import tpu as pltpu

D_MODEL = 1024
CHUNK = 128
D_GMLP = 1024
GMLP_GROUPS = 8
N_HEADS = 16
KV_HEADS = 2
Q_PER_KV = N_HEADS // KV_HEADS
HEAD_DIM = 64
D_ATTN = N_HEADS * HEAD_DIM
D_KV = KV_HEADS * HEAD_DIM
CMP_BLOCK = 32
CMP_STRIDE = 16
CMP_HIDDEN = 2 * HEAD_DIM
SEL_BLOCK = 64
N_SELECT = 16
WINDOW = 512
Q_BLOCK = 128
FORCE_BONUS = 1e4
N_KV_SLOTS = 4
N_EXPERTS = 16
N_EXPERT_GROUPS = 4
EXPERTS_PER_GROUP = 4
D_EXPERT = 512
PAGE_SIZE = 128
PROJ_SIZES = (D_GMLP, D_GMLP, D_ATTN) + (D_KV,) * 6 + (3 * N_HEADS, D_MODEL, D_MODEL)

LANES = 128
SUBLANES = 8
VMEM_LIMIT_BYTES = 56 * 1024 * 1024

NEG = -1e30
LOG2E = 1.4426950408889634
KEY_TILE = 2 * Q_BLOCK
COL_CHUNK = 2 * LANES
CMP_VIS_ROWS = 128
BIAS_ROWS = 16
ONES_ROWS = 16
BF = jnp.bfloat16
F32 = jnp.float32
ROUTE_W = LANES
XR_W = D_MODEL + ROUTE_W
ROW_DMA_UNROLL = 8


def _cparams(sem):
    return pltpu.CompilerParams(dimension_semantics=sem, vmem_limit_bytes=VMEM_LIMIT_BYTES)


def _layer_norm(x, g, b, eps=1e-5):
    mu = jnp.mean(x, -1, keepdims=True)
    xc = x - mu
    var = jnp.mean(xc * xc, -1, keepdims=True)
    return xc * lax.rsqrt(var + eps) * g + b


def _dot(a, b):
    return jnp.dot(a, b, preferred_element_type=F32)


def _dot_nt(a, b):
    return lax.dot_general(a, b, (((1,), (1,)), ((), ())), preferred_element_type=F32)


def _dot_tn(a, b):
    return lax.dot_general(a, b, (((0,), (0,)), ((), ())), preferred_element_type=F32)


def _full(shape):
    nd = len(shape)
    return pl.BlockSpec(shape, lambda *_: (0,) * nd)


def _gmlp_kernel(x_ref, wu_ref, wv_ref, wga_ref, wpa_ref, lng_ref, lnb_ref, ws_ref, bs_ref,
                 ya_ref, *rest, sample):
    xb = x_ref[...].astype(BF)
    u = jax.nn.gelu(_dot(xb, wu_ref[...]))
    v = jax.nn.gelu(_dot(xb, wv_ref[...]))
    vn = _layer_norm(v, lng_ref[...], lnb_ref[...])
    tm = xb.shape[0]
    if sample:
        a = u * (vn * ws_ref[...] + bs_ref[...])
        rest[0][...] = vn
    else:
        n_chunk = tm // CHUNK
        vb = vn.astype(BF)
        row = lax.broadcasted_iota(jnp.int32, (CHUNK, CHUNK), 0)
        col = lax.broadcasted_iota(jnp.int32, (CHUNK, CHUNK), 1)
        causal = col <= row
        bs = bs_ref[...]
        cols = []
        for g in range(GMLP_GROUPS):
            lo = g * CHUNK
            wg = jnp.where(causal, ws_ref[g], 0.0).astype(BF)
            vg = jnp.concatenate([vb[c * CHUNK:(c + 1) * CHUNK, lo:lo + CHUNK] for c in range(n_chunk)], axis=1)
            mixed = _dot(wg, vg) + bs[:, g:g + 1]
            cols.append(jnp.concatenate([mixed[:, c * CHUNK:(c + 1) * CHUNK] for c in range(n_chunk)], axis=0))
        a = u * jnp.concatenate(cols, axis=1)
    ga = jax.nn.sigmoid(_dot(xb, wga_ref[...]))
    ya_ref[...] = ga * _dot(a.astype(BF), wpa_ref[...])


def _gmlp_call(x, wu, wv, wga, wpa, lng, lnb, ws, bs, *, tm, sample):
    n = x.shape[0]
    wspec = _full((D_MODEL, D_MODEL))
    vspec = _full((1, D_MODEL))
    in_specs = [pl.BlockSpec((tm, D_MODEL), lambda i: (i, 0)), wspec, wspec, wspec, wspec, vspec, vspec,
                _full(ws.shape), _full(bs.shape)]
    tok = pl.BlockSpec((tm, D_MODEL), lambda i: (i, 0))
    out_shape = [jax.ShapeDtypeStruct((n, D_MODEL), F32)]
    out_specs = [tok]
    if sample:
        out_shape.append(jax.ShapeDtypeStruct((n, D_MODEL), F32))
        out_specs.append(tok)
    return pl.pallas_call(
        functools.partial(_gmlp_kernel, sample=sample),
        grid=(n // tm,), in_specs=in_specs, out_specs=out_specs, out_shape=out_shape,
        compiler_params=_cparams(("arbitrary",)), name="gmlp_sample" if sample else "gmlp_prompt",
    )(x, wu, wv, wga, wpa, lng, lnb, ws, bs)


def _attn_in_prompt_kernel(x_ref, wgb_ref, wkv_ref, wqt_ref, wkvt_ref, wgnt_ref,
                           sgb_ref, kv_ref, kvb_ref, kvt_ref, qt_ref, gnt_ref):
    xb = x_ref[...].astype(BF)
    sgb_ref[...] = jax.nn.sigmoid(_dot(xb, wgb_ref[...]))
    kv = _dot(xb, wkv_ref[...])
    kv_ref[...] = kv
    kvb_ref[...] = kv.astype(BF)
    kvt_ref[...] = _dot_nt(wkvt_ref[...], xb).astype(BF)
    qt_ref[...] = (_dot_nt(wqt_ref[...], xb) * (LOG2E * HEAD_DIM ** -0.5)).astype(BF)
    gnt_ref[...] = jax.nn.sigmoid(_dot_nt(wgnt_ref[...], xb))


def _attn_in_prompt_call(x, wgb, wkv, wqt, wkvt, wgnt, *, tm):
    n = x.shape[0]
    nkv = wkv.shape[1]
    ngn = wgnt.shape[0]
    in_specs = [pl.BlockSpec((tm, D_MODEL), lambda i: (i, 0)), _full(wgb.shape), _full(wkv.shape),
                _full(wqt.shape), _full(wkvt.shape), _full(wgnt.shape)]
    out_shape = [jax.ShapeDtypeStruct((n, D_MODEL), F32), jax.ShapeDtypeStruct((n, nkv), F32),
                 jax.ShapeDtypeStruct((n, nkv), BF), jax.ShapeDtypeStruct((nkv, n), BF),
                 jax.ShapeDtypeStruct((D_ATTN, n), BF), jax.ShapeDtypeStruct((ngn, n), F32)]
    out_specs = [pl.BlockSpec((tm, D_MODEL), lambda i: (i, 0)), pl.BlockSpec((tm, nkv), lambda i: (i, 0)),
                 pl.BlockSpec((tm, nkv), lambda i: (i, 0)), pl.BlockSpec((nkv, tm), lambda i: (0, i)),
                 pl.BlockSpec((D_ATTN, tm), lambda i: (0, i)), pl.BlockSpec((ngn, tm), lambda i: (0, i))]
    return pl.pallas_call(
        _attn_in_prompt_kernel, grid=(n // tm,), in_specs=in_specs, out_specs=out_specs, out_shape=out_shape,
        compiler_params=_cparams(("arbitrary",)), name="attn_in_prompt",
    )(x, wgb, wkv, wqt, wkvt, wgnt)


def _attn_in_sample_kernel(x_ref, wgb_ref, wkv_ref, wq_ref, wgn_ref, sgb_ref, kv_ref, q_ref, gn_ref):
    xb = x_ref[...].astype(BF)
    sgb_ref[...] = jax.nn.sigmoid(_dot(xb, wgb_ref[...]))
    kv_ref[...] = _dot(xb, wkv_ref[...])
    q_ref[...] = _dot(xb, wq_ref[...]) * (HEAD_DIM ** -0.5)
    gn_ref[...] = jax.nn.sigmoid(_dot(xb, wgn_ref[...]))


def _attn_in_sample_call(x, wgb, wkv, wq, wgn):
    n = x.shape[0]
    nkv = wkv.shape[1]
    out_shape = [jax.ShapeDtypeStruct((n, D_MODEL), F32), jax.ShapeDtypeStruct((n, nkv), F32),
                 jax.ShapeDtypeStruct((n, D_ATTN), F32), jax.ShapeDtypeStruct((n, wgn.shape[1]), F32)]
    return pl.pallas_call(
        _attn_in_sample_kernel, grid=(1,),
        in_specs=[_full(x.shape), _full(wgb.shape), _full(wkv.shape), _full(wq.shape), _full(wgn.shape)],
        out_specs=[_full(s.shape) for s in out_shape], out_shape=out_shape,
        compiler_params=_cparams(("arbitrary",)), name="attn_in_sample",
    )(x, wgb, wkv, wq, wgn)


def _compress_rows(src_refs, n_sub, wp_ref, w2p_ref, w2pt_ref, ppos_ref):
    outs = []
    for slot in range(2):
        src_ref = src_refs[slot]
        x = jnp.concatenate(
            [src_ref[pl.ds(s, n_sub, stride=CMP_STRIDE), :].astype(BF) for s in range(CMP_STRIDE)],
            axis=1)
        hh = _dot(x, wp_ref[slot])
        pp = _dot(ppos_ref[slot], wp_ref[slot])
        pos = pp[0:1, 0:2 * CMP_HIDDEN] + pp[1:2, 2 * CMP_HIDDEN:]
        h1 = pltpu.roll(hh[:, 2 * CMP_HIDDEN:], n_sub - 1, 0)
        g = jax.nn.gelu(hh[:, 0:2 * CMP_HIDDEN] + h1 + pos).astype(BF)
        if slot == 0:
            outs.append(_dot(g, w2p_ref[0]))
        else:
            outs.append(_dot_nt(w2pt_ref[1], g))
    return outs


def _compress_kernel(kcm_ref, vcm_ref, wp_ref, w2p_ref, w2pt_ref, ppos_ref, kc_ref, vct_ref):
    n_sub = kc_ref.shape[0]
    kc, vct = _compress_rows((kcm_ref, vcm_ref), n_sub, wp_ref, w2p_ref, w2pt_ref, ppos_ref)
    kc_ref[...] = kc.astype(BF)
    vct_ref[...] = vct.astype(BF)


def _compress_call(kv, cw, *, batch, seq):
    n_sub = seq // CMP_STRIDE
    wp, w2p, w2pt, ppos = cw
    return pl.pallas_call(
        _compress_kernel, grid=(batch,),
        in_specs=[pl.BlockSpec((seq, LANES), lambda b: (b, 0)), pl.BlockSpec((seq, LANES), lambda b: (b, 1)),
                  _full(wp.shape), _full(w2p.shape), _full(w2pt.shape), _full(ppos.shape)],
        out_specs=[pl.BlockSpec((None, n_sub, LANES), lambda b: (b, 0, 0)),
                   pl.BlockSpec((None, LANES, n_sub), lambda b: (b, 0, 0))],
        out_shape=[jax.ShapeDtypeStruct((batch, n_sub, LANES), BF), jax.ShapeDtypeStruct((batch, LANES, n_sub), BF)],
        compiler_params=_cparams(("arbitrary",)), name="compress_prompt",
    )(kv, kv, wp, w2p, w2pt, ppos)


def _top_blocks(score):
    n_blk = score.shape[0]
    jf = lax.broadcasted_iota(jnp.int32, score.shape, 0).astype(F32)
    work = score
    sel = jnp.zeros_like(score)
    for _ in range(N_SELECT):
        mx = jnp.max(work, axis=0, keepdims=True)
        first = jnp.min(jnp.where(work == mx, jf, float(n_blk)), axis=0, keepdims=True)
        pick = jf == first
        sel = jnp.where(pick, 1.0, sel)
        work = jnp.where(pick, -3e38, work)
    return sel


def _nsa_prompt_kernel(qt_ref, gn_ref, ksel_ref, kwin_ref, vselt_ref, vwint_ref, kc_ref, vct_ref, ovt_ref, eblk_ref,
                       o_ref, sel_ref, bias_ref, m_ref, acc_ref, qtp_ref, tot_ref, sa_ref, sb_ref):
    i = pl.program_id(1)
    q0 = i * Q_BLOCK
    seq = ksel_ref.shape[0]
    n_cmp = kc_ref.shape[0]
    n_selb = ovt_ref.shape[0]
    nq = Q_PER_KV * Q_BLOCK
    qpos = q0 + lax.broadcasted_iota(jnp.int32, (1, Q_BLOCK), 1)
    blk =lax.broadcasted_iota(jnp.int32, (n_selb, Q_BLOCK), 0)
    cur = qpos // SEL_BLOCK
    forced = (blk == 0) | (blk == cur) | (blk == cur - 1)
    valid = blk * SEL_BLOCK <= qpos
    krow = lax.broadcasted_iota(jnp.int32, (KEY_TILE, Q_BLOCK), 0)
    tcol = lax.broadcasted_iota(jnp.int32, (KEY_TILE, Q_BLOCK), 1)
    zeros_q = jnp.zeros((HEAD_DIM, nq), BF)
    ones = jnp.ones((ONES_ROWS, KEY_TILE), BF)

    chunks = [(h, c) for h in range(KV_HEADS) for c in range(nq // COL_CHUNK)]

    def chunk_cols(c):
        return slice(c * COL_CHUNK, (c + 1) * COL_CHUNK)

    blocks_per_tile = KEY_TILE // SEL_BLOCK
    phantom_bias = n_selb // blocks_per_tile

    def set_bias(bias_idx):
        rows = [bias_ref[pl.ds(bias_idx * blocks_per_tile + jb, 1), :] for jb in range(blocks_per_tile)]
        tile = jnp.concatenate(rows + [jnp.zeros((BIAS_ROWS - blocks_per_tile, KV_HEADS * Q_BLOCK), F32)], axis=0)
        for h in range(KV_HEADS):
            qtp_ref[h, LANES:LANES + BIAS_ROWS, :] = jnp.concatenate(
                [tile[:, h * Q_BLOCK:(h + 1) * Q_BLOCK]] * Q_PER_KV, axis=1).astype(BF)

    def score_chunk(dst_ref, k_tile, h, c):
        k_aug = jnp.concatenate([k_tile, eblk_ref[...]], axis=1)
        dst_ref[h, :, chunk_cols(c)] = _dot(k_aug, qtp_ref[h, :, chunk_cols(c)])

    def flash_tile(src_ref, dst_ref, k_next, vt_tiles, masks, next_bias=None):
        vt_augs = [jnp.concatenate([vt, ones], axis=0) for vt in vt_tiles]
        if next_bias is not None:
            set_bias(next_bias)
        for h, c in chunks:
            cols = chunk_cols(c)
            if k_next is not None:
                score_chunk(dst_ref, k_next, h, c)
            m_old = m_ref[h, :, cols]
            ps, m_news = [], []
            for g in range(COL_CHUNK // Q_BLOCK):
                lo = c * COL_CHUNK + g * Q_BLOCK
                s = src_ref[h, :, lo:lo + Q_BLOCK]
                if masks is not None:
                    s = jnp.where(masks[h], s, NEG)
                m_new = jnp.maximum(m_old[:, g * Q_BLOCK:(g + 1) * Q_BLOCK], jnp.max(s, axis=0, keepdims=True))
                ps.append(jnp.exp2(s - m_new).astype(BF))
                m_news.append(m_new)
            m_new = jnp.concatenate(m_news, axis=1)
            acc_ref[h, :, cols] = (jnp.exp2(m_old - m_new) * acc_ref[h, :, cols]
                                   + _dot(vt_augs[h], jnp.concatenate(ps, axis=1)))
            m_ref[h, :, cols] = m_new

    def flash_branch(k_ref, vt_ref, n_tiles, tile_start, tile_masks, tile_bias=None):
        def k_tile(t):
            k0 = pl.multiple_of(jnp.clip(tile_start(t), 0, seq - KEY_TILE), Q_BLOCK)
            return k_ref[pl.ds(k0, KEY_TILE), :]

        def vt_tiles(t):
            k0 = pl.multiple_of(jnp.clip(tile_start(t), 0, seq - KEY_TILE), Q_BLOCK)
            return [head_rows(vt_ref, h)[:, pl.ds(k0, KEY_TILE)] for h in range(KV_HEADS)]

        m_ref[...] = jnp.full(m_ref.shape, NEG, F32)
        acc_ref[...] = jnp.zeros(acc_ref.shape, F32)
        n_loop = n_tiles if tile_bias is None else n_tiles - 1

        def first_scores(t, bias_idx):
            if bias_idx is not None:
                set_bias(bias_idx)
            first = k_tile(t)
            for h, c in chunks:
                score_chunk(sa_ref, first, h, c)

        def loop_masks(t):
            return tile_masks(t) if tile_bias is None else None

        def loop_bias(t):
            return None if tile_bias is None else jnp.where(t < n_loop, tile_bias(t), phantom_bias)

        first_scores(0, loop_bias(0))

        def body(j, carry):
            flash_tile(sa_ref, sb_ref, k_tile(2 * j + 1), vt_tiles(2 * j), loop_masks(2 * j), loop_bias(2 * j + 1))
            flash_tile(sb_ref, sa_ref, k_tile(2 * j + 2), vt_tiles(2 * j + 1), loop_masks(2 * j + 1), loop_bias(2 * j + 2))
            return carry

        lax.fori_loop(0, (n_loop + 1) // 2, body, 0)
        if tile_bias is not None:
            last = n_tiles - 1
            first_scores(last, tile_bias(last))
            flash_tile(sa_ref, None, None, vt_tiles(last), tile_masks(last))

    def head_rows(ref, h):
        return ref.at[h * HEAD_DIM:(h + 1) * HEAD_DIM]

    def flash_result(h):
        return acc_ref[h, 0:HEAD_DIM, :] * (1.0 / acc_ref[h, HEAD_DIM:HEAD_DIM + 1, :])

    def gate(h, c):
        r = (h * 3 + c) * Q_PER_KV
        return jnp.concatenate([gn_ref[r + g:r + g + 1, :] for g in range(Q_PER_KV)], axis=1)

    for h in range(KV_HEADS):
        qh = qt_ref[h * Q_PER_KV * HEAD_DIM:(h + 1) * Q_PER_KV * HEAD_DIM, :]
        qcat = jnp.concatenate([qh[g * HEAD_DIM:(g + 1) * HEAD_DIM, :] for g in range(Q_PER_KV)], axis=1)
        qtp = jnp.concatenate([qcat, zeros_q] if h == 0 else [zeros_q, qcat], axis=0)
        qtp_ref[h, 0:LANES, :] = qtp
        qtp_ref[h, LANES:2 * LANES, :] = jnp.zeros((LANES, nq), BF)

    def compressed_branch(n_rows):
        cmp_end = lax.broadcasted_iota(jnp.int32, (n_rows, Q_BLOCK), 0) * CMP_STRIDE + (CMP_BLOCK - 1)
        vis = cmp_end <= qpos
        any_vis = qpos >= CMP_BLOCK - 1
        for h in range(KV_HEADS):
            sc = _dot(kc_ref[0:n_rows, :], qtp_ref[h, 0:LANES, :])
            psum = jnp.zeros((n_rows, Q_BLOCK), F32)
            pcols = []
            for g in range(Q_PER_KV):
                s = jnp.where(vis, sc[:, g * Q_BLOCK:(g + 1) * Q_BLOCK], NEG)
                e = jnp.exp2(s - jnp.max(s, axis=0, keepdims=True))
                inv = jnp.where(any_vis, 1.0 / jnp.sum(e, axis=0, keepdims=True), 0.0)
                p = e * inv
                psum = psum + p
                pcols.append(p.astype(BF))
            tot_ref[h] = gate(h, 0) * _dot(vct_ref[h * HEAD_DIM:(h + 1) * HEAD_DIM, 0:n_rows],
                                           jnp.concatenate(pcols, axis=1))
            p_hi = psum.astype(BF)
            p_lo = (psum - p_hi.astype(F32)).astype(BF)
            imp = _dot(ovt_ref[:, 0:n_rows], p_hi) + _dot(ovt_ref[:, 0:n_rows], p_lo)
            sel_ref[:, h * Q_BLOCK:(h + 1) * Q_BLOCK] = jnp.where(valid, imp + jnp.where(forced, FORCE_BONUS, 0.0), NEG)

    vis_step = min(n_cmp, CMP_VIS_ROWS)
    last_vis = (q0 + Q_BLOCK - CMP_BLOCK) // CMP_STRIDE
    for c in range(n_cmp // vis_step):
        pl.when(last_vis // vis_step == c)(functools.partial(compressed_branch, (c + 1) * vis_step))

    picked = (_top_blocks(sel_ref[...]) > 0.5) & jnp.concatenate([valid] * KV_HEADS, axis=1)
    bias_ref[0:n_selb, :] = jnp.where(picked, 0.0, NEG)
    bias_ref[n_selb:n_selb + SUBLANES, :] = jnp.full((SUBLANES, KV_HEADS * Q_BLOCK), NEG, F32)

    def sel_masks(t):
        return [t * KEY_TILE + krow <= q0 + tcol] * KV_HEADS

    flash_branch(ksel_ref, vselt_ref, (q0 + Q_BLOCK + KEY_TILE - 1) // KEY_TILE, lambda t: t * KEY_TILE, sel_masks,
                 tile_bias=lambda t: t)
    for h in range(KV_HEADS):
        tot_ref[h] = tot_ref[h] + gate(h, 1) * flash_result(h)
        qtp_ref[h, LANES:LANES + BIAS_ROWS, :] = jnp.zeros((BIAS_ROWS, nq), BF)

    n_wt = (WINDOW + Q_BLOCK + KEY_TILE - 1) // KEY_TILE
    w0 = jnp.clip(q0 - WINDOW, 0, seq - n_wt * KEY_TILE)

    def win_masks(t):
        rel = (q0 + tcol) - (w0 + t * KEY_TILE + krow)
        return [(rel >= 0) & (rel < WINDOW)] * KV_HEADS

    flash_branch(kwin_ref, vwint_ref, n_wt, lambda t: w0 + t * KEY_TILE, win_masks)

    for h in range(KV_HEADS):
        tot = tot_ref[h] + gate(h, 2) * flash_result(h)
        for g in range(Q_PER_KV):
            r = (h * Q_PER_KV + g) * HEAD_DIM
            o_ref[r:r + HEAD_DIM, :] = tot[:, g * Q_BLOCK:(g + 1) * Q_BLOCK].astype(BF)


def _nsa_prompt_call(qt, gnt, kvb, kvt, kc, vct, ovt, *, batch, seq):
    nblk = seq // Q_BLOCK
    eblk = jnp.asarray(np.arange(LANES)[None, :] == (np.arange(KEY_TILE) // SEL_BLOCK)[:, None], BF)
    n = batch * seq
    n_sub = seq // CMP_STRIDE
    n_selb = seq // SEL_BLOCK
    nq = Q_PER_KV * Q_BLOCK
    in_specs = [
        pl.BlockSpec((D_ATTN, Q_BLOCK), lambda b, i: (0, b * nblk + i)),
        pl.BlockSpec((gnt.shape[0], Q_BLOCK), lambda b, i: (0, b * nblk + i)),
        pl.BlockSpec((seq, LANES), lambda b, i: (b, 2)),
        pl.BlockSpec((seq, LANES), lambda b, i: (b, 4)),
        pl.BlockSpec((LANES, seq), lambda b, i: (3, b)),
        pl.BlockSpec((LANES, seq), lambda b, i: (5, b)),
        pl.BlockSpec((None, n_sub, LANES), lambda b, i: (b, 0, 0)),
        pl.BlockSpec((None, LANES, n_sub), lambda b, i: (b, 0, 0)),
        _full(ovt.shape),
        _full(eblk.shape),
    ]
    return pl.pallas_call(
        _nsa_prompt_kernel, grid=(batch, nblk), in_specs=in_specs,
        out_specs=pl.BlockSpec((D_ATTN, Q_BLOCK), lambda b, i: (0, b * nblk + i)),
        out_shape=jax.ShapeDtypeStruct((D_ATTN, n), BF),
        scratch_shapes=[pltpu.VMEM((n_selb, KV_HEADS * Q_BLOCK), F32),
                        pltpu.VMEM((n_selb + SUBLANES, KV_HEADS * Q_BLOCK), F32), pltpu.VMEM((KV_HEADS, 1, nq), F32),
                        pltpu.VMEM((KV_HEADS, HEAD_DIM + ONES_ROWS, nq), F32),
                        pltpu.VMEM((KV_HEADS, 2 * LANES, nq), BF),
                        pltpu.VMEM((KV_HEADS, HEAD_DIM, nq), F32),
                        pltpu.VMEM((KV_HEADS, KEY_TILE, nq), F32), pltpu.VMEM((KV_HEADS, KEY_TILE, nq), F32)],
        compiler_params=_cparams(("arbitrary", "arbitrary")), name="nsa_prompt",
    )(qt, gnt, kvb, kvb, kvt, kvt, kc, vct, ovt, eblk)


def _route(x1, wrt_ref, br_ref, tri_ref, cnt_ref):
    tm = x1.shape[0]
    logits = lax.dot_general(wrt_ref[...], x1, (((1,), (1,)), ((), ())), precision=lax.Precision.HIGHEST,
                             preferred_element_type=F32)
    aff = jax.nn.sigmoid(logits)
    grp = aff + br_ref[...]
    affr = [aff[k:k + 1, :] for k in range(N_EXPERTS)]
    grpr = [grp[k:k + 1, :] for k in range(N_EXPERTS)]
    best = None
    gsel = jnp.zeros((1, tm), jnp.int32)
    for gi in range(N_EXPERT_GROUPS):
        m = grpr[gi * 4:(gi + 1) * 4]
        top2 = None
        for a in range(4):
            for b in range(a + 1, 4):
                s = m[a] + m[b]
                top2 = s if top2 is None else jnp.maximum(top2, s)
        if best is None:
            best = top2
        else:
            better = top2 > best
            gsel = jnp.where(better, gi, gsel)
            best = jnp.where(better, top2, best)
    ing, ina = [], []
    for j in range(4):
        vg, va = grpr[j], affr[j]
        for gi in range(1, N_EXPERT_GROUPS):
            vg = jnp.where(gsel == gi, grpr[gi * 4 + j], vg)
            va = jnp.where(gsel == gi, affr[gi * 4 + j], va)
        ing.append(vg)
        ina.append(va)

    def argmax4(vals):
        bv, bi = vals[0], jnp.zeros((1, tm), jnp.int32)
        for j in range(1, 4):
            better = vals[j] > bv
            bi = jnp.where(better, j, bi)
            bv = jnp.where(better, vals[j], bv)
        return bi

    loc1 = argmax4(ing)
    loc2 = argmax4([jnp.where(loc1 == j, -3e38, ing[j]) for j in range(4)])
    w1 = sum(jnp.where(loc1 == j, ina[j], 0.0) for j in range(4))
    w2 = sum(jnp.where(loc2 == j, ina[j], 0.0) for j in range(4))
    tot = w1 + w2
    wd = [jnp.where(loc1 == j, w1 / tot, 0.0) + jnp.where(loc2 == j, w2 / tot, 0.0) for j in range(4)]
    oh = jnp.concatenate([(gsel == gi).astype(F32) for gi in range(N_EXPERT_GROUPS)]
                         + [jnp.zeros((SUBLANES - N_EXPERT_GROUPS, tm), F32)], axis=0)
    cum = _dot(oh.astype(BF), tri_ref[...])
    carry = cnt_ref[...][:, 0:1]
    rank = jnp.sum(oh * (cum + carry), axis=0, keepdims=True)
    cnt_ref[...] = cnt_ref[...] + jnp.sum(oh, axis=1, keepdims=True)
    rows = wd + [gsel.astype(F32), rank, jnp.zeros((ROUTE_W - 6, tm), F32)]
    return jnp.concatenate(rows, axis=0)


def _merge_kernel(x_ref, ya_ref, sgb_ref, o_ref, wpb_ref, wo_ref, g_ref, b_ref, wrt_ref, br_ref, tri_ref,
                  xr_ref, cnt_out_ref, cnt_ref, *, alpha, o_transposed):
    @pl.when(pl.program_id(0) == 0)
    def _():
        cnt_ref[...] = jnp.zeros(cnt_ref.shape, F32)

    if o_transposed:
        ob = _dot_tn(o_ref[...], wpb_ref[...])
    else:
        ob = _dot(o_ref[...].astype(BF), wpb_ref[...])
    y = ya_ref[...] + sgb_ref[...] * ob
    mix = _dot(y.astype(BF), wo_ref[...])
    x1 = _layer_norm(alpha * x_ref[...] + mix, g_ref[...], b_ref[...])
    xr_ref[:, 0:D_MODEL] = x1
    info = _route(x1, wrt_ref, br_ref, tri_ref, cnt_ref)
    xr_ref[:, D_MODEL:XR_W] = info.T
    cnt_out_ref[...] = cnt_ref[...]


def _merge_call(x, ya, sgb, o, wpb, wo, g, b, wrt, br, tri, *, tm, alpha, o_transposed):
    n = ya.shape[0]
    tok = pl.BlockSpec((tm, D_MODEL), lambda i: (i, 0))
    ospec = pl.BlockSpec((D_ATTN, tm), lambda i: (0, i)) if o_transposed else tok
    return pl.pallas_call(
        functools.partial(_merge_kernel, alpha=alpha, o_transposed=o_transposed), grid=(n // tm,),
        in_specs=[tok, tok, tok, ospec, _full(wpb.shape), _full(wo.shape), _full(g.shape), _full(b.shape),
                  _full(wrt.shape), _full(br.shape), _full(tri.shape)],
        out_specs=[pl.BlockSpec((tm, XR_W), lambda i: (i, 0)), _full((SUBLANES, LANES))],
        out_shape=[jax.ShapeDtypeStruct((n, XR_W), F32), jax.ShapeDtypeStruct((SUBLANES, LANES), F32)],
        scratch_shapes=[pltpu.VMEM((SUBLANES, LANES), F32)],
        compiler_params=_cparams(("arbitrary",)), name="merge_t" if o_transposed else "merge_n",
    )(x, ya, sgb, o, wpb, wo, g, b, wrt, br, tri)


def _row_copy(src_ref, dst_ref, sem, src_row, dst_row):
    return pltpu.make_async_copy(src_ref.at[pl.ds(src_row, 1), :], dst_ref.at[pl.ds(dst_row, 1), :], sem)


def _scatter_kernel(dest_ref, xr_ref, xs_in_ref, xs_ref, sem):
    del xs_in_ref
    tm = xr_ref.shape[0]

    def start(r, c):
        _row_copy(xr_ref, xs_ref, sem, r, dest_ref[r]).start()
        return c

    def wait(r, c):
        _row_copy(xr_ref, xs_ref, sem, r, dest_ref[r]).wait()
        return c

    lax.fori_loop(0, tm, start, 0, unroll=ROW_DMA_UNROLL)
    lax.fori_loop(0, tm, wait, 0, unroll=ROW_DMA_UNROLL)


def _scatter_call(dest, xr, xs0, *, tm):
    n = xr.shape[0]
    return pl.pallas_call(
        _scatter_kernel, grid=(n // tm,),
        in_specs=[pl.BlockSpec((tm,), lambda i: (i,), memory_space=pltpu.SMEM),
                  pl.BlockSpec((tm, XR_W), lambda i: (i, 0)), pl.BlockSpec(memory_space=pl.ANY)],
        out_specs=pl.BlockSpec(memory_space=pl.ANY),
        out_shape=jax.ShapeDtypeStruct(xs0.shape, F32),
        scratch_shapes=[pltpu.SemaphoreType.DMA(())],
        input_output_aliases={2: 0},
        compiler_params=_cparams(("arbitrary",)), name="moe_scatter",
    )(dest, xr, xs0)


def _experts_kernel(tg_ref, nu_ref, xs_ref, wg_ref, wu_ref, wd_ref, ys_ref):
    del tg_ref
    j = pl.program_id(0)

    @pl.when(j < nu_ref[0])
    def _():
        xb = xs_ref[:, 0:D_MODEL].astype(BF)
        acc = None
        for e in range(EXPERTS_PER_GROUP):
            hidden = jax.nn.silu(_dot(xb, wg_ref[e])) * _dot(xb, wu_ref[e])
            hidden = hidden * xs_ref[:, D_MODEL + e:D_MODEL + e + 1]
            y = _dot(hidden.astype(BF), wd_ref[e])
            acc = y if acc is None else acc + y
        ys_ref[...] = acc

    @pl.when(j >= nu_ref[0])
    def _():
        ys_ref[...] = jnp.zeros(ys_ref.shape, F32)


def _experts_call(tile_group, n_used, xs, wg, wu, wd, *, tmx):
    rows = xs.shape[0]
    wspec_in = pl.BlockSpec((None, EXPERTS_PER_GROUP, D_MODEL, D_EXPERT), lambda j, tg, nu: (tg[j], 0, 0, 0))
    wspec_out = pl.BlockSpec((None, EXPERTS_PER_GROUP, D_EXPERT, D_MODEL), lambda j, tg, nu: (tg[j], 0, 0, 0))
    grid_spec = pltpu.PrefetchScalarGridSpec(
        num_scalar_prefetch=2, grid=(rows // tmx,),
        in_specs=[pl.BlockSpec((tmx, XR_W), lambda j, tg, nu: (j, 0)), wspec_in, wspec_in, wspec_out],
        out_specs=pl.BlockSpec((tmx, D_MODEL), lambda j, tg, nu: (j, 0)))
    return pl.pallas_call(
        _experts_kernel, grid_spec=grid_spec, out_shape=jax.ShapeDtypeStruct((rows, D_MODEL), F32),
        compiler_params=_cparams(("arbitrary",)), name="moe_experts",
    )(tile_group, n_used, xs, wg, wu, wd)


def _combine_kernel(dest_ref, xr_ref, ys_ref, g_ref, b_ref, x2_ref, buf_ref, sem, *, alpha):
    tm = xr_ref.shape[0]

    def start(r, c):
        _row_copy(ys_ref, buf_ref, sem, dest_ref[r], r).start()
        return c

    def wait(r, c):
        _row_copy(ys_ref, buf_ref, sem, dest_ref[r], r).wait()
        return c

    lax.fori_loop(0, tm, start, 0, unroll=ROW_DMA_UNROLL)
    lax.fori_loop(0, tm, wait, 0, unroll=ROW_DMA_UNROLL)
    x2_ref[...] = _layer_norm(alpha * xr_ref[...] + buf_ref[...], g_ref[...], b_ref[...])


def _combine_call(dest, xr, ys, g, b, *, tm, alpha):
    n = xr.shape[0]
    return pl.pallas_call(
        functools.partial(_combine_kernel, alpha=alpha), grid=(n // tm,),
        in_specs=[pl.BlockSpec((tm,), lambda i: (i,), memory_space=pltpu.SMEM),
                  pl.BlockSpec((tm, D_MODEL), lambda i: (i, 0)), pl.BlockSpec(memory_space=pl.ANY),
                  _full(g.shape), _full(b.shape)],
        out_specs=pl.BlockSpec((tm, D_MODEL), lambda i: (i, 0)),
        out_shape=jax.ShapeDtypeStruct((n, D_MODEL), F32),
        scratch_shapes=[pltpu.VMEM((tm, D_MODEL), F32), pltpu.SemaphoreType.DMA(())],
        compiler_params=_cparams(("arbitrary",)), name="moe_combine",
    )(dest, xr, ys, g, b)


def _moe(xr, cnt, lw, *, tm, tmx, alpha):
    n = xr.shape[0]
    n_tiles = n // tmx + N_EXPERT_GROUPS
    counts = cnt[:N_EXPERT_GROUPS, 0].astype(jnp.int32)
    tiles_per = (counts + tmx - 1) // tmx
    tile_end = jnp.cumsum(tiles_per)
    offs = (tile_end - tiles_per) * tmx
    gid = xr[:, D_MODEL + 4].astype(jnp.int32)
    rank = xr[:, D_MODEL + 5].astype(jnp.int32)
    dest = offs[gid] + rank
    tile_group = jnp.minimum(jnp.sum(jnp.arange(n_tiles)[:, None] >= tile_end[None, :], axis=1),
                             N_EXPERT_GROUPS - 1).astype(jnp.int32)
    n_used = tile_end[-1:].astype(jnp.int32)
    xs = _scatter_call(dest, xr, jnp.zeros((n_tiles * tmx, XR_W), F32), tm=tm)
    ys = _experts_call(tile_group, n_used, xs, lw["wg"], lw["wu_e"], lw["wd"], tmx=tmx)
    return _combine_call(dest, xr, ys, lw["ln2_g"], lw["ln2_b"], tm=tm, alpha=alpha)


def _softmax_with_new(s, mask, s_new, new_on):
    s = jnp.where(mask, s, NEG)
    s_new = jnp.where(new_on, s_new, NEG)
    mx = jnp.maximum(jnp.max(s, axis=1, keepdims=True), s_new)
    p = jnp.where(mask, jnp.exp(s - mx), 0.0)
    p_new = jnp.where(new_on, jnp.exp(s_new - mx), 0.0)
    den = jnp.maximum(jnp.sum(p, axis=1, keepdims=True) + p_new, 1e-30)
    return p, p_new, 1.0 / den


def _sample_attention(qs, news, gates, cw_refs, x_refs, sel_refs, wp_ref, w2p_ref, w2pt_ref, ppos_ref, ov_ref,
                      exp_ref, *, n_past):
    n_seq = len(qs)
    seqs = range(n_seq)
    n_sub = n_past // CMP_STRIDE
    n_cmp = n_sub - CMP_BLOCK // CMP_STRIDE + 1
    nrow = KV_HEADS * Q_PER_KV
    qpos = n_past
    qfs = [q.astype(F32) for q in qs]

    def new_score(k, col):
        kn = news[k][:, col * LANES:(col + 1) * LANES].astype(BF).astype(F32)
        return jnp.sum(qfs[k] * kn, axis=1, keepdims=True)

    def new_value(k, col):
        return news[k][:, col * LANES:(col + 1) * LANES].astype(BF).astype(F32)

    def rows(x, idx, n):
        return x[idx * n:(idx + 1) * n]

    g_all = []
    for slot in range(2):
        w = wp_ref[slot]
        pp = _dot(ppos_ref[slot], w)
        pos = pp[0:1, 0:2 * CMP_HIDDEN] + pp[1:2, 2 * CMP_HIDDEN:]
        x_all = jnp.concatenate(
            [jnp.concatenate([x_refs[k][slot, pl.ds(s, n_sub, stride=CMP_STRIDE), :].astype(BF)
                              for s in range(CMP_STRIDE)], axis=1) for k in seqs], axis=0)
        hh_all = _dot(x_all, w)
        g_seq = []
        for k in seqs:
            hh = rows(hh_all, k, n_sub)
            h1 = pltpu.roll(hh[:, 2 * CMP_HIDDEN:], n_sub - 1, 0)
            g_seq.append(jax.nn.gelu(hh[:, 0:2 * CMP_HIDDEN] + h1 + pos).astype(BF))
        g_all.append(jnp.concatenate(g_seq, axis=0))
    kc_all = _dot(g_all[0], w2p_ref[0]).astype(BF)
    vct_all = _dot_nt(w2pt_ref[1], g_all[1]).astype(BF)

    n_idx = lax.broadcasted_iota(jnp.int32, (nrow, n_sub), 1)
    vis = (n_idx * CMP_STRIDE + (CMP_BLOCK - 1) <= qpos) & (n_idx < n_cmp)
    s_cs = [_dot_nt(qs[k], rows(kc_all, k, n_sub)) for k in seqs]
    p_cs = []
    for k in seqs:
        s_c = jnp.where(vis, s_cs[k], NEG)
        mx = jnp.max(s_c, axis=1, keepdims=True)
        p_c = jnp.where(vis, jnp.exp(s_c - mx), 0.0)
        p_cs.append(p_c * (1.0 / jnp.maximum(jnp.sum(p_c, axis=1, keepdims=True), 1e-30)))
    o_cs = [_dot_nt(p_cs[k].astype(BF), vct_all[:, k * n_sub:(k + 1) * n_sub]) for k in seqs]

    psum = jnp.concatenate([jnp.sum(p_cs[k][h * Q_PER_KV:(h + 1) * Q_PER_KV], axis=0, keepdims=True)
                            for k in seqs for h in range(KV_HEADS)], axis=0)
    imp_rows = -(-psum.shape[0] // SUBLANES) * SUBLANES
    if imp_rows > psum.shape[0]:
        psum = jnp.concatenate([psum, jnp.zeros((imp_rows - psum.shape[0], n_sub), F32)], axis=0)
    p_hi = psum.astype(BF)
    p_lo = (psum - p_hi.astype(F32)).astype(BF)
    imp = _dot(p_hi, ov_ref[...]) + _dot(p_lo, ov_ref[...])
    n_selp = imp.shape[1]
    j_idx = lax.broadcasted_iota(jnp.int32, (imp_rows, n_selp), 1)
    cur = qpos // SEL_BLOCK
    forced = (j_idx == 0) | (j_idx == cur) | (j_idx == cur - 1)
    valid = j_idx * SEL_BLOCK <= qpos
    score = jnp.where(valid, imp + jnp.where(forced, FORCE_BONUS, 0.0), NEG)
    score_t = jnp.concatenate([score, jnp.zeros((n_selp - imp_rows, n_selp), F32)], axis=0).T
    jr = lax.broadcasted_iota(jnp.int32, (n_selp, n_selp), 0)
    jc = lax.broadcasted_iota(jnp.int32, (n_selp, n_selp), 1)
    sel_rows = []
    for r in range(n_seq * KV_HEADS):
        other = jnp.broadcast_to(score_t[:, r:r + 1], (n_selp, n_selp))
        mine = jnp.broadcast_to(score[r:r + 1, :], (n_selp, n_selp))
        ahead = (other > mine) | ((other == mine) & (jr < jc))
        rank = jnp.sum(ahead.astype(F32), axis=0, keepdims=True)
        sel_rows.append(jnp.broadcast_to((rank < N_SELECT).astype(F32), (Q_PER_KV, n_selp)))
    sel = jnp.concatenate(sel_rows, axis=0)

    on_all = _dot(sel.astype(BF), exp_ref[...]) > 0.5
    s_ss = [_dot_nt(qs[k], sel_refs[k][:, 0:LANES].astype(BF)) for k in seqs]
    soft = [_softmax_with_new(s_ss[k], rows(on_all, k, nrow), new_score(k, 2), rows(sel, k, nrow)[:, cur:cur + 1] > 0.5)
            for k in seqs]
    pv_s = [_dot(soft[k][0].astype(BF), sel_refs[k][:, LANES:2 * LANES].astype(BF)) for k in seqs]
    o_ss = [(pv_s[k] + soft[k][1] * new_value(k, 3)) * soft[k][2] for k in seqs]

    wb = cw_refs[0].shape[0]
    wpos = (n_past - wb) + lax.broadcasted_iota(jnp.int32, (nrow, wb), 1)
    rel = qpos - wpos
    wmask = (rel >= 0) & (rel < WINDOW) & (wpos >= 0)
    s_ws = [_dot_nt(qs[k], cw_refs[k][:, 0:LANES]) for k in seqs]
    soft = [_softmax_with_new(s_ws[k], wmask, new_score(k, 4), jnp.full((nrow, 1), True)) for k in seqs]
    pv_w = [_dot(soft[k][0].astype(BF), cw_refs[k][:, LANES:2 * LANES]) for k in seqs]
    o_ws = [(pv_w[k] + soft[k][1] * new_value(k, 5)) * soft[k][2] for k in seqs]

    return [gates[k][:, 0:1] * o_cs[k] + gates[k][:, 1:2] * o_ss[k] + gates[k][:, 2:3] * o_ws[k] for k in seqs]


SEQ_PER_STEP = 4


def _nsa_sample_kernel(pt_ref, q_ref, new_ref, gate_ref, cw_ref, cache_ref, wp_ref, w2p_ref, w2pt_ref,
                       ppos_ref, ov_ref, exp_ref, o_ref, x_buf, sel_buf, sems, *, layer, n_pages, n_past):
    j = pl.program_id(0)
    n_steps = pl.num_programs(0)
    cur = lax.rem(j, 2)

    def step_copies(buf_set, step):
        copies = []
        for k in range(SEQ_PER_STEP):
            for p in range(n_pages):
                page = cache_ref.at[pt_ref[(step * SEQ_PER_STEP + k) * n_pages + p], layer]
                dst_rows = pl.ds(p * PAGE_SIZE, PAGE_SIZE)
                for slot in range(2):
                    copies.append(pltpu.make_async_copy(page.at[:, pl.ds(slot * LANES, LANES)],
                                                        x_buf.at[buf_set, k, slot, dst_rows, :], sems.at[buf_set]))
                copies.append(pltpu.make_async_copy(page.at[:, pl.ds(2 * LANES, 2 * LANES)],
                                                    sel_buf.at[buf_set, k, dst_rows, :], sems.at[buf_set]))
        return copies

    @pl.when(j == 0)
    def _():
        for c in step_copies(cur, j):
            c.start()

    @pl.when(j + 1 < n_steps)
    def _():
        for c in step_copies(1 - cur, j + 1):
            c.start()

    for c in step_copies(cur, j):
        c.wait()
    ks = range(SEQ_PER_STEP)
    outs = _sample_attention([q_ref[k] for k in ks], [new_ref[k] for k in ks], [gate_ref[k] for k in ks],
                             [cw_ref.at[k] for k in ks], [x_buf.at[cur, k] for k in ks],
                             [sel_buf.at[cur, k] for k in ks], wp_ref, w2p_ref, w2pt_ref, ppos_ref, ov_ref, exp_ref,
                             n_past=n_past)
    for k in ks:
        o_ref[k] = outs[k]


def _nsa_sample_call(pt, qp, new, gates, cwin, cache, cw, ov, expand, *, layer, n_past):
    n_dec = qp.shape[0]
    n_pages = n_past // PAGE_SIZE
    nrow = KV_HEADS * Q_PER_KV
    wb = cwin.shape[2]
    wp, w2p, w2pt, ppos = cw
    const = lambda shape: pl.BlockSpec(shape, lambda b, pt_: (0,) * len(shape))
    per_seq = lambda *tail: pl.BlockSpec((SEQ_PER_STEP,) + tail, lambda b, pt_: (b,) + (0,) * len(tail))
    grid_spec = pltpu.PrefetchScalarGridSpec(
        num_scalar_prefetch=1, grid=(n_dec // SEQ_PER_STEP,),
        in_specs=[per_seq(nrow, LANES), per_seq(1, new.shape[2]), per_seq(nrow, LANES),
                  pl.BlockSpec((SEQ_PER_STEP, None, wb, 2 * LANES), lambda b, pt_: (b, layer, 0, 0)),
                  pl.BlockSpec(memory_space=pl.ANY),
                  const(wp.shape), const(w2p.shape), const(w2pt.shape), const(ppos.shape), const(ov.shape),
                  const(expand.shape)],
        out_specs=per_seq(nrow, LANES),
        scratch_shapes=[pltpu.VMEM((2, SEQ_PER_STEP, 2, n_past, LANES), F32),
                        pltpu.VMEM((2, SEQ_PER_STEP, n_past, 2 * LANES), F32),
                        pltpu.SemaphoreType.DMA((2,))])
    return pl.pallas_call(
        functools.partial(_nsa_sample_kernel, layer=layer, n_pages=n_pages, n_past=n_past),
        grid_spec=grid_spec, out_shape=jax.ShapeDtypeStruct((n_dec, nrow, LANES), F32),
        compiler_params=_cparams(("arbitrary",)), name="nsa_sample",
    )(pt, qp, new, gates, cwin, cache, wp, w2p, w2pt, ppos, ov, expand)


def _overlap(n_cmp, n_sel):
    c0 = np.arange(n_cmp) * CMP_STRIDE
    s0 = np.arange(n_sel) * SEL_BLOCK
    return ((c0[:, None] < s0[None, :] + SEL_BLOCK) & (c0[:, None] + CMP_BLOCK > s0[None, :])).astype(np.float32)


def _layer_weights(l, w_in, gmlp_ln_g, gmlp_ln_b, gmlp_ws, gmlp_bs, cmp_pos, cmp_w1, cmp_w2, w_branch_a, w_branch_b,
                   w_out, ln1_g, ln1_b, ln2_g, ln2_b, w_gate, w_up, w_down):
    cuts = np.cumsum((0,) + PROJ_SIZES)
    seg = lambda i, j=None: w_in[l][:, cuts[i]:cuts[i + 1 if j is None else j]].astype(BF)
    wgn = seg(9)
    eye = jnp.eye(KV_HEADS, dtype=F32)
    r = CMP_BLOCK // CMP_STRIDE
    w1r = cmp_w1[l].reshape(2, r, CMP_STRIDE, HEAD_DIM, CMP_HIDDEN)
    wp = jnp.einsum("xrsdf,hk->xshdrkf", w1r, eye).reshape(2, CMP_STRIDE * D_KV, r * KV_HEADS * CMP_HIDDEN)
    w2p = jnp.einsum("xfd,hk->xhfkd", cmp_w2[l], eye).reshape(2, KV_HEADS * CMP_HIDDEN, D_KV)
    pos = cmp_pos[l].reshape(2, r, CMP_STRIDE, 1, HEAD_DIM)
    ppos = jnp.broadcast_to(pos, (2, r, CMP_STRIDE, KV_HEADS, HEAD_DIM)).reshape(2, r, CMP_STRIDE * D_KV)
    ppos = jnp.pad(ppos, ((0, 0), (0, SUBLANES - r), (0, 0)))
    ex = lambda w: w.reshape((N_EXPERT_GROUPS, EXPERTS_PER_GROUP) + w.shape[1:]).astype(BF)
    row = lambda v: v[l][None, :].astype(F32)
    return {
        "wu": seg(0), "wv": seg(1), "wq": seg(2), "wkv": seg(3, 9), "wga": seg(10), "wgb": seg(11),
        "wqt": seg(2).T, "wkvt": seg(3, 9).T,
        "wgnt": wgn.reshape(D_MODEL, KV_HEADS, Q_PER_KV, 3).transpose(1, 3, 2, 0).reshape(3 * N_HEADS, D_MODEL),
        "wgn": jnp.pad(wgn, ((0, 0), (0, LANES - 3 * N_HEADS))),
        "gln_g": row(gmlp_ln_g), "gln_b": row(gmlp_ln_b),
        "ws": gmlp_ws[l], "bs_t": gmlp_bs[l].T,
        "ws0": jnp.repeat(gmlp_ws[l][:, 0, 0], CHUNK)[None, :], "bs0": jnp.repeat(gmlp_bs[l][:, 0], CHUNK)[None, :],
        "cw": (wp.astype(BF), w2p.astype(BF), w2p.transpose(0, 2, 1).astype(BF), ppos.astype(BF)),
        "wpa": w_branch_a[l].astype(BF), "wpb": w_branch_b[l].astype(BF), "wo": w_out[l].astype(BF),
        "ln1_g": row(ln1_g), "ln1_b": row(ln1_b), "ln2_g": row(ln2_g), "ln2_b": row(ln2_b),
        "wg": ex(w_gate[l]), "wu_e": ex(w_up[l]), "wd": ex(w_down[l]),
    }


def _strict_upper(n):
    return jnp.asarray(np.triu(np.ones((n, n), np.float32), 1), BF)


def kernel(x_prompt, x_sample, cache_kv, cache_win, page_table, w_in, gmlp_ln_g, gmlp_ln_b, gmlp_ws, gmlp_bs,
           cmp_pos, cmp_w1, cmp_w2, w_branch_a, w_branch_b, w_out, ln1_g, ln1_b, ln2_g, ln2_b,
           w_router, b_router, w_gate, w_up, w_down):
    depth = w_in.shape[0]
    alpha = (2.0 * depth) ** 0.25
    batch, seq, _ = x_prompt.shape
    n_dec, t_dec = x_sample.shape[:2]
    n_past = page_table.shape[1] * PAGE_SIZE
    assert t_dec == 1 and seq % (4 * CHUNK) == 0 and n_past % PAGE_SIZE == 0 and n_dec % SUBLANES == 0
    tm_p, tmx_p = 4 * CHUNK, 2 * CHUNK
    tm_s = tmx_s = n_dec
    wb = cache_win.shape[2]

    n_sub_p = seq // CMP_STRIDE
    ovt_p = jnp.asarray(np.pad(_overlap(n_sub_p - 1, seq // SEL_BLOCK), ((0, 1), (0, 0))).T, BF)
    n_sub_s = n_past // CMP_STRIDE
    n_sel_s = n_past // SEL_BLOCK + 1
    ov_s = jnp.asarray(np.pad(_overlap(n_sub_s - 1, n_sel_s), ((0, 1), (0, LANES - n_sel_s))), BF)
    expand = jnp.asarray(np.arange(LANES)[:, None] == (np.arange(n_past) // SEL_BLOCK)[None, :], BF)
    wrt = w_router.T.astype(F32)
    br = b_router[:, None].astype(F32)
    tri_p, tri_s = _strict_upper(tm_p), _strict_upper(tm_s)
    cache = cache_kv.reshape(cache_kv.shape[0], depth, PAGE_SIZE, N_KV_SLOTS * D_KV)
    cwin = cache_win.reshape(n_dec, depth, wb, 2 * D_KV).astype(BF)
    pt = page_table.reshape(-1).astype(jnp.int32)

    xp = x_prompt.reshape(batch * seq, D_MODEL)
    xs = x_sample.reshape(n_dec, D_MODEL)
    kv_p, kv_s, win_s, gv_s = [], [], [], []
    for l in range(depth):
        lw = _layer_weights(l, w_in, gmlp_ln_g, gmlp_ln_b, gmlp_ws, gmlp_bs, cmp_pos, cmp_w1, cmp_w2, w_branch_a,
                            w_branch_b, w_out, ln1_g, ln1_b, ln2_g, ln2_b, w_gate, w_up, w_down)
        (ya,) = _gmlp_call(xp, lw["wu"], lw["wv"], lw["wga"], lw["wpa"], lw["gln_g"], lw["gln_b"], lw["ws"],
                           lw["bs_t"], tm=tm_p, sample=False)
        sgb, kv, kvb, kvt, qt, gnt = _attn_in_prompt_call(xp, lw["wgb"], lw["wkv"], lw["wqt"], lw["wkvt"], lw["wgnt"],
                                                          tm=tm_p)
        kc, vct = _compress_call(kv, lw["cw"], batch=batch, seq=seq)
        ot = _nsa_prompt_call(qt, gnt, kvb, kvt, kc, vct, ovt_p, batch=batch, seq=seq)
        xr, cnt = _merge_call(xp, ya, sgb, ot, lw["wpb"], lw["wo"], lw["ln1_g"], lw["ln1_b"], wrt, br, tri_p,
                              tm=tm_p, alpha=alpha, o_transposed=True)
        xp = _moe(xr, cnt, lw, tm=tm_p, tmx=tmx_p, alpha=alpha)
        kv_p.append(kv)
        ya, vn = _gmlp_call(xs, lw["wu"], lw["wv"], lw["wga"], lw["wpa"], lw["gln_g"], lw["gln_b"], lw["ws0"],
                            lw["bs0"], tm=tm_s, sample=True)
        sgb, kv, q, gn = _attn_in_sample_call(xs, lw["wgb"], lw["wkv"], lw["wq"], lw["wgn"])
        qh = q.reshape(n_dec, KV_HEADS, Q_PER_KV, HEAD_DIM)
        zq = jnp.zeros_like(qh[:, 0])
        qp = jnp.concatenate([jnp.concatenate([qh[:, 0], zq], -1), jnp.concatenate([zq, qh[:, 1]], -1)], 1).astype(BF)
        gates = jnp.pad(gn[:, :3 * N_HEADS].reshape(n_dec, N_HEADS, 3), ((0, 0), (0, 0), (0, LANES - 3)))
        o16 = _nsa_sample_call(pt, qp, kv[:, None, :], gates, cwin, cache, lw["cw"], ov_s, expand, layer=l,
                               n_past=n_past)
        o = jnp.concatenate([o16[:, :Q_PER_KV, :HEAD_DIM], o16[:, Q_PER_KV:, HEAD_DIM:]], 1).reshape(n_dec, D_ATTN)
        xr, cnt = _merge_call(xs, ya, sgb, o, lw["wpb"], lw["wo"], lw["ln1_g"], lw["ln1_b"], wrt, br, tri_s,
                              tm=tm_s, alpha=alpha, o_transposed=False)
        xs = _moe(xr, cnt, lw, tm=tm_s, tmx=tmx_s, alpha=alpha)
        kvr = kv.reshape(n_dec, 1, 6, KV_HEADS, HEAD_DIM)
        kv_s.append(kvr[:, :, :N_KV_SLOTS])
        win_s.append(jnp.concatenate([cache_win[:, l], kvr[:, :, N_KV_SLOTS:]], axis=1)[:, -wb:])
        gv_s.append(vn[:, None, :])
    kv_all = jnp.stack(kv_p, axis=0).reshape(depth, batch, seq, 6, KV_HEADS, HEAD_DIM)
    kv_rows_prompt = kv_all[:, :, :, :N_KV_SLOTS].transpose(1, 0, 2, 3, 4, 5)
    win_prompt = kv_all[:, :, seq - min(WINDOW, seq):, N_KV_SLOTS:].transpose(1, 0, 2, 3, 4, 5)
    return (xp.reshape(batch, seq, D_MODEL), xs.reshape(n_dec, 1, D_MODEL), kv_rows_prompt, win_prompt,
            jnp.stack(kv_s, axis=1), jnp.stack(win_s, axis=1), jnp.stack(gv_s, axis=1))
```

```python
import functools

import numpy as np
import jax
import jax.numpy as jnp
from jax import lax
from jax.experimental import pallas as pl
from jax.experimental.pallas import tpu as pltpu

D_MODEL = 1024
CHUNK = 128
D_GMLP = 1024
GMLP_GROUPS = 8
N_HEADS = 16
KV_HEADS = 2
Q_PER_KV = N_HEADS // KV_HEADS
HEAD_DIM = 64
D_ATTN = N_HEADS * HEAD_DIM
D_KV = KV_HEADS * HEAD_DIM
CMP_BLOCK = 32
CMP_STRIDE = 16
CMP_HIDDEN = 2 * HEAD_DIM
SEL_BLOCK = 64
N_SELECT = 16
WINDOW = 512
Q_BLOCK = 128
FORCE_BONUS = 1e4
N_KV_SLOTS = 4
N_EXPERTS = 16
N_EXPERT_GROUPS = 4
EXPERTS_PER_GROUP = 4
D_EXPERT = 512
PAGE_SIZE = 128
PROJ_SIZES = (D_GMLP, D_GMLP, D_ATTN) + (D_KV,) * 6 + (3 * N_HEADS, D_MODEL, D_MODEL)

LANES = 128
SUBLANES = 8
VMEM_LIMIT_BYTES = 56 * 1024 * 1024

NEG = -1e30
LOG2E = 1.4426950408889634
KEY_TILE = 4 * Q_BLOCK
COL_CHUNK = 2 * LANES
CMP_VIS_ROWS = 128
BIAS_ROWS = 16
ONES_ROWS = 16
BF = jnp.bfloat16
F32 = jnp.float32
ROUTE_W = LANES
XR_W = D_MODEL + ROUTE_W
ROW_DMA_UNROLL = 8


def _cparams(sem):
    return pltpu.CompilerParams(dimension_semantics=sem, vmem_limit_bytes=VMEM_LIMIT_BYTES)


def _layer_norm(x, g, b, eps=1e-5):
    mu = jnp.mean(x, -1, keepdims=True)
    xc = x - mu
    var = jnp.mean(xc * xc, -1, keepdims=True)
    return xc * lax.rsqrt(var + eps) * g + b


def _dot(a, b):
    return jnp.dot(a, b, preferred_element_type=F32)


def _dot_nt(a, b):
    return lax.dot_general(a, b, (((1,), (1,)), ((), ())), preferred_element_type=F32)


def _dot_tn(a, b):
    return lax.dot_general(a, b, (((0,), (0,)), ((), ())), preferred_element_type=F32)


def _full(shape):
    nd = len(shape)
    return pl.BlockSpec(shape, lambda *_: (0,) * nd)


def _gmlp_kernel(x_ref, wu_ref, wv_ref, wga_ref, wpa_ref, lng_ref, lnb_ref, ws_ref, bs_ref,
                 ya_ref, *rest, sample):
    xb = x_ref[...].astype(BF)
    u = jax.nn.gelu(_dot(xb, wu_ref[...]))
    v = jax.nn.gelu(_dot(xb, wv_ref[...]))
    vn = _layer_norm(v, lng_ref[...], lnb_ref[...])
    tm = xb.shape[0]
    if sample:
        a = u * (vn * ws_ref[...] + bs_ref[...])
        rest[0][...] = vn
    else:
        n_chunk = tm // CHUNK
        vb = vn.astype(BF)
        row = lax.broadcasted_iota(jnp.int32, (CHUNK, CHUNK), 0)
        col = lax.broadcasted_iota(jnp.int32, (CHUNK, CHUNK), 1)
        causal = col <= row
        bs = bs_ref[...]
        cols = []
        for g in range(GMLP_GROUPS):
            lo = g * CHUNK
            wg = jnp.where(causal, ws_ref[g], 0.0).astype(BF)
            vg = jnp.concatenate([vb[c * CHUNK:(c + 1) * CHUNK, lo:lo + CHUNK] for c in range(n_chunk)], axis=1)
            mixed = _dot(wg, vg) + bs[:, g:g + 1]
            cols.append(jnp.concatenate([mixed[:, c * CHUNK:(c + 1) * CHUNK] for c in range(n_chunk)], axis=0))
        a = u * jnp.concatenate(cols, axis=1)
    ga = jax.nn.sigmoid(_dot(xb, wga_ref[...]))
    ya_ref[...] = ga * _dot(a.astype(BF), wpa_ref[...])


def _gmlp_call(x, wu, wv, wga, wpa, lng, lnb, ws, bs, *, tm, sample):
    n = x.shape[0]
    wspec = _full((D_MODEL, D_MODEL))
    vspec = _full((1, D_MODEL))
    in_specs = [pl.BlockSpec((tm, D_MODEL), lambda i: (i, 0)), wspec, wspec, wspec, wspec, vspec, vspec,
                _full(ws.shape), _full(bs.shape)]
    tok = pl.BlockSpec((tm, D_MODEL), lambda i: (i, 0))
    out_shape = [jax.ShapeDtypeStruct((n, D_MODEL), F32)]
    out_specs = [tok]
    if sample:
        out_shape.append(jax.ShapeDtypeStruct((n, D_MODEL), F32))
        out_specs.append(tok)
    return pl.pallas_call(
        functools.partial(_gmlp_kernel, sample=sample),
        grid=(n // tm,), in_specs=in_specs, out_specs=out_specs, out_shape=out_shape,
        compiler_params=_cparams(("arbitrary",)), name="gmlp_sample" if sample else "gmlp_prompt",
    )(x, wu, wv, wga, wpa, lng, lnb, ws, bs)


def _attn_in_prompt_kernel(x_ref, wgb_ref, wkv_ref, wqt_ref, wkvt_ref, wgnt_ref,
                           sgb_ref, kv_ref, kvb_ref, kvt_ref, qt_ref, gnt_ref):
    xb = x_ref[...].astype(BF)
    sgb_ref[...] = jax.nn.sigmoid(_dot(xb, wgb_ref[...]))
    kv = _dot(xb, wkv_ref[...])
    kv_ref[...] = kv
    kvb_ref[...] = kv.astype(BF)
    kvt_ref[...] = _dot_nt(wkvt_ref[...], xb).astype(BF)
    qt_ref[...] = (_dot_nt(wqt_ref[...], xb) * (LOG2E * HEAD_DIM ** -0.5)).astype(BF)
    gnt_ref[...] = jax.nn.sigmoid(_dot_nt(wgnt_ref[...], xb))


def _attn_in_prompt_call(x, wgb, wkv, wqt, wkvt, wgnt, *, tm):
    n = x.shape[0]
    nkv = wkv.shape[1]
    ngn = wgnt.shape[0]
    in_specs = [pl.BlockSpec((tm, D_MODEL), lambda i: (i, 0)), _full(wgb.shape), _full(wkv.shape),
                _full(wqt.shape), _full(wkvt.shape), _full(wgnt.shape)]
    out_shape = [jax.ShapeDtypeStruct((n, D_MODEL), F32), jax.ShapeDtypeStruct((n, nkv), F32),
                 jax.ShapeDtypeStruct((n, nkv), BF), jax.ShapeDtypeStruct((nkv, n), BF),
                 jax.ShapeDtypeStruct((D_ATTN, n), BF), jax.ShapeDtypeStruct((ngn, n), F32)]
    out_specs = [pl.BlockSpec((tm, D_MODEL), lambda i: (i, 0)), pl.BlockSpec((tm, nkv), lambda i: (i, 0)),
                 pl.BlockSpec((tm, nkv), lambda i: (i, 0)), pl.BlockSpec((nkv, tm), lambda i: (0, i)),
                 pl.BlockSpec((D_ATTN, tm), lambda i: (0, i)), pl.BlockSpec((ngn, tm), lambda i: (0, i))]
    return pl.pallas_call(
        _attn_in_prompt_kernel, grid=(n // tm,), in_specs=in_specs, out_specs=out_specs, out_shape=out_shape,
        compiler_params=_cparams(("arbitrary",)), name="attn_in_prompt",
    )(x, wgb, wkv, wqt, wkvt, wgnt)


def _attn_in_sample_kernel(x_ref, wgb_ref, wkv_ref, wq_ref, wgn_ref, sgb_ref, kv_ref, q_ref, gn_ref):
    xb = x_ref[...].astype(BF)
    sgb_ref[...] = jax.nn.sigmoid(_dot(xb, wgb_ref[...]))
    kv_ref[...] = _dot(xb, wkv_ref[...])
    q_ref[...] = _dot(xb, wq_ref[...]) * (HEAD_DIM ** -0.5)
    gn_ref[...] = jax.nn.sigmoid(_dot(xb, wgn_ref[...]))


def _attn_in_sample_call(x, wgb, wkv, wq, wgn):
    n = x.shape[0]
    nkv = wkv.shape[1]
    out_shape = [jax.ShapeDtypeStruct((n, D_MODEL), F32), jax.ShapeDtypeStruct((n, nkv), F32),
                 jax.ShapeDtypeStruct((n, D_ATTN), F32), jax.ShapeDtypeStruct((n, wgn.shape[1]), F32)]
    return pl.pallas_call(
        _attn_in_sample_kernel, grid=(1,),
        in_specs=[_full(x.shape), _full(wgb.shape), _full(wkv.shape), _full(wq.shape), _full(wgn.shape)],
        out_specs=[_full(s.shape) for s in out_shape], out_shape=out_shape,
        compiler_params=_cparams(("arbitrary",)), name="attn_in_sample",
    )(x, wgb, wkv, wq, wgn)


def _compress_rows(src_refs, n_sub, wp_ref, w2p_ref, w2pt_ref, ppos_ref):
    outs = []
    for slot in range(2):
        src_ref = src_refs[slot]
        x = jnp.concatenate(
            [src_ref[pl.ds(s, n_sub, stride=CMP_STRIDE), :].astype(BF) for s in range(CMP_STRIDE)],
            axis=1)
        hh = _dot(x, wp_ref[slot])
        pp = _dot(ppos_ref[slot], wp_ref[slot])
        pos = pp[0:1, 0:2 * CMP_HIDDEN] + pp[1:2, 2 * CMP_HIDDEN:]
        h1 = pltpu.roll(hh[:, 2 * CMP_HIDDEN:], n_sub - 1, 0)
        g = jax.nn.gelu(hh[:, 0:2 * CMP_HIDDEN] + h1 + pos).astype(BF)
        if slot == 0:
            outs.append(_dot(g, w2p_ref[0]))
        else:
            outs.append(_dot_nt(w2pt_ref[1], g))
    return outs


def _compress_kernel(kcm_ref, vcm_ref, wp_ref, w2p_ref, w2pt_ref, ppos_ref, kc_ref, vct_ref):
    n_sub = kc_ref.shape[0]
    kc, vct = _compress_rows((kcm_ref, vcm_ref), n_sub, wp_ref, w2p_ref, w2pt_ref, ppos_ref)
    kc_ref[...] = kc.astype(BF)
    vct_ref[...] = vct.astype(BF)


def _compress_call(kv, cw, *, batch, seq):
    n_sub = seq // CMP_STRIDE
    wp, w2p, w2pt, ppos = cw
    return pl.pallas_call(
        _compress_kernel, grid=(batch,),
        in_specs=[pl.BlockSpec((seq, LANES), lambda b: (b, 0)), pl.BlockSpec((seq, LANES), lambda b: (b, 1)),
                  _full(wp.shape), _full(w2p.shape), _full(w2pt.shape), _full(ppos.shape)],
        out_specs=[pl.BlockSpec((None, n_sub, LANES), lambda b: (b, 0, 0)),
                   pl.BlockSpec((None, LANES, n_sub), lambda b: (b, 0, 0))],
        out_shape=[jax.ShapeDtypeStruct((batch, n_sub, LANES), BF), jax.ShapeDtypeStruct((batch, LANES, n_sub), BF)],
        compiler_params=_cparams(("arbitrary",)), name="compress_prompt",
    )(kv, kv, wp, w2p, w2pt, ppos)


def _top_blocks(score):
    n_blk = score.shape[0]
    jf = lax.broadcasted_iota(jnp.int32, score.shape, 0).astype(F32)
    work = score
    sel = jnp.zeros_like(score)
    for _ in range(N_SELECT):
        mx = jnp.max(work, axis=0, keepdims=True)
        first = jnp.min(jnp.where(work == mx, jf, float(n_blk)), axis=0, keepdims=True)
        pick = jf == first
        sel = jnp.where(pick, 1.0, sel)
        work = jnp.where(pick, -3e38, work)
    return sel


def _nsa_prompt_kernel(qt_ref, gn_ref, ksel_ref, kwin_ref, vselt_ref, vwint_ref, kc_ref, vct_ref, ovt_ref, eblk_ref,
                       o_ref, sel_ref, bias_ref, m_ref, acc_ref, qtp_ref, tot_ref, sa_ref, sb_ref):
    i = pl.program_id(1)
    q0 = i * Q_BLOCK
    seq = ksel_ref.shape[0]
    n_cmp = kc_ref.shape[0]
    n_selb = ovt_ref.shape[0]
    nq = Q_PER_KV * Q_BLOCK
    qpos = q0 + lax.broadcasted_iota(jnp.int32, (1, Q_BLOCK), 1)
    blk =lax.broadcasted_iota(jnp.int32, (n_selb, Q_BLOCK), 0)
    cur = qpos // SEL_BLOCK
    forced = (blk == 0) | (blk == cur) | (blk == cur - 1)
    valid = blk * SEL_BLOCK <= qpos
    krow = lax.broadcasted_iota(jnp.int32, (KEY_TILE, Q_BLOCK), 0)
    tcol = lax.broadcasted_iota(jnp.int32, (KEY_TILE, Q_BLOCK), 1)
    zeros_q = jnp.zeros((HEAD_DIM, nq), BF)
    ones = jnp.ones((ONES_ROWS, KEY_TILE), BF)

    chunks = [(h, c) for h in range(KV_HEADS) for c in range(nq // COL_CHUNK)]

    def chunk_cols(c):
        return slice(c * COL_CHUNK, (c + 1) * COL_CHUNK)

    blocks_per_tile = KEY_TILE // SEL_BLOCK
    phantom_bias = n_selb // blocks_per_tile

    def set_bias(bias_idx):
        rows = [bias_ref[pl.ds(bias_idx * blocks_per_tile + jb, 1), :] for jb in range(blocks_per_tile)]
        tile = jnp.concatenate(rows + [jnp.zeros((BIAS_ROWS - blocks_per_tile, KV_HEADS * Q_BLOCK), F32)], axis=0)
        for h in range(KV_HEADS):
            qtp_ref[h, LANES:LANES + BIAS_ROWS, :] = jnp.concatenate(
                [tile[:, h * Q_BLOCK:(h + 1) * Q_BLOCK]] * Q_PER_KV, axis=1).astype(BF)

    def score_chunk(dst_ref, k_tile, h, c):
        k_aug = jnp.concatenate([k_tile, eblk_ref[...]], axis=1)
        dst_ref[h, :, chunk_cols(c)] = _dot(k_aug, qtp_ref[h, :, chunk_cols(c)])

    def flash_tile(src_ref, dst_ref, k_next, vt_tiles, masks, next_bias=None):
        vt_augs = [jnp.concatenate([vt, ones], axis=0) for vt in vt_tiles]
        if next_bias is not None:
            set_bias(next_bias)
        for h, c in chunks:
            cols = chunk_cols(c)
            if k_next is not None:
                score_chunk(dst_ref, k_next, h, c)
            m_old = m_ref[h, :, cols]
            ps, m_news = [], []
            for g in range(COL_CHUNK // Q_BLOCK):
                lo = c * COL_CHUNK + g * Q_BLOCK
                s = src_ref[h, :, lo:lo + Q_BLOCK]
                if masks is not None:
                    s = jnp.where(masks[h], s, NEG)
                m_new = jnp.maximum(m_old[:, g * Q_BLOCK:(g + 1) * Q_BLOCK], jnp.max(s, axis=0, keepdims=True))
                ps.append(jnp.exp2(s - m_new).astype(BF))
                m_news.append(m_new)
            m_new = jnp.concatenate(m_news, axis=1)
            acc_ref[h, :, cols] = (jnp.exp2(m_old - m_new) * acc_ref[h, :, cols]
                                   + _dot(vt_augs[h], jnp.concatenate(ps, axis=1)))
            m_ref[h, :, cols] = m_new

    def flash_branch(k_ref, vt_ref, n_tiles, tile_start, tile_masks, tile_bias=None):
        def k_tile(t):
            k0 = pl.multiple_of(jnp.clip(tile_start(t), 0, seq - KEY_TILE), Q_BLOCK)
            return k_ref[pl.ds(k0, KEY_TILE), :]

        def vt_tiles(t):
            k0 = pl.multiple_of(jnp.clip(tile_start(t), 0, seq - KEY_TILE), Q_BLOCK)
            return [head_rows(vt_ref, h)[:, pl.ds(k0, KEY_TILE)] for h in range(KV_HEADS)]

        m_ref[...] = jnp.full(m_ref.shape, NEG, F32)
        acc_ref[...] = jnp.zeros(acc_ref.shape, F32)
        n_loop = n_tiles if tile_bias is None else n_tiles - 1

        def first_scores(t, bias_idx):
            if bias_idx is not None:
                set_bias(bias_idx)
            first = k_tile(t)
            for h, c in chunks:
                score_chunk(sa_ref, first, h, c)

        def loop_masks(t):
            return tile_masks(t) if tile_bias is None else None

        def loop_bias(t):
            return None if tile_bias is None else jnp.where(t < n_loop, tile_bias(t), phantom_bias)

        first_scores(0, loop_bias(0))

        def body(j, carry):
            flash_tile(sa_ref, sb_ref, k_tile(2 * j + 1), vt_tiles(2 * j), loop_masks(2 * j), loop_bias(2 * j + 1))
            flash_tile(sb_ref, sa_ref, k_tile(2 * j + 2), vt_tiles(2 * j + 1), loop_masks(2 * j + 1), loop_bias(2 * j + 2))
            return carry

        lax.fori_loop(0, (n_loop + 1) // 2, body, 0)
        if tile_bias is not None:
            last = n_tiles - 1
            first_scores(last, tile_bias(last))
            flash_tile(sa_ref, None, None, vt_tiles(last), tile_masks(last))

    def head_rows(ref, h):
        return ref.at[h * HEAD_DIM:(h + 1) * HEAD_DIM]

    def flash_result(h):
        return acc_ref[h, 0:HEAD_DIM, :] * (1.0 / acc_ref[h, HEAD_DIM:HEAD_DIM + 1, :])

    def gate(h, c):
        r = (h * 3 + c) * Q_PER_KV
        return jnp.concatenate([gn_ref[r + g:r + g + 1, :] for g in range(Q_PER_KV)], axis=1)

    for h in range(KV_HEADS):
        qh = qt_ref[h * Q_PER_KV * HEAD_DIM:(h + 1) * Q_PER_KV * HEAD_DIM, :]
        qcat = jnp.concatenate([qh[g * HEAD_DIM:(g + 1) * HEAD_DIM, :] for g in range(Q_PER_KV)], axis=1)
        qtp = jnp.concatenate([qcat, zeros_q] if h == 0 else [zeros_q, qcat], axis=0)
        qtp_ref[h, 0:LANES, :] = qtp
        qtp_ref[h, LANES:2 * LANES, :] = jnp.zeros((LANES, nq), BF)

    def compressed_branch(n_rows):
        cmp_end = lax.broadcasted_iota(jnp.int32, (n_rows, Q_BLOCK), 0) * CMP_STRIDE + (CMP_BLOCK - 1)
        vis = cmp_end <= qpos
        any_vis = qpos >= CMP_BLOCK - 1
        for h in range(KV_HEADS):
            sc = _dot(kc_ref[0:n_rows, :], qtp_ref[h, 0:LANES, :])
            psum = jnp.zeros((n_rows, Q_BLOCK), F32)
            pcols = []
            for g in range(Q_PER_KV):
                s = jnp.where(vis, sc[:, g * Q_BLOCK:(g + 1) * Q_BLOCK], NEG)
                e = jnp.exp2(s - jnp.max(s, axis=0, keepdims=True))
                inv = jnp.where(any_vis, 1.0 / jnp.sum(e, axis=0, keepdims=True), 0.0)
                p = e * inv
                psum = psum + p
                pcols.append(p.astype(BF))
            tot_ref[h] = gate(h, 0) * _dot(vct_ref[h * HEAD_DIM:(h + 1) * HEAD_DIM, 0:n_rows],
                                           jnp.concatenate(pcols, axis=1))
            p_hi = psum.astype(BF)
            p_lo = (psum - p_hi.astype(F32)).astype(BF)
            imp = _dot(ovt_ref[:, 0:n_rows], p_hi) + _dot(ovt_ref[:, 0:n_rows], p_lo)
            sel_ref[:, h * Q_BLOCK:(h + 1) * Q_BLOCK] = jnp.where(valid, imp + jnp.where(forced, FORCE_BONUS, 0.0), NEG)

    vis_step = min(n_cmp, CMP_VIS_ROWS)
    last_vis = (q0 + Q_BLOCK - CMP_BLOCK) // CMP_STRIDE
    for c in range(n_cmp // vis_step):
        pl.when(last_vis // vis_step == c)(functools.partial(compressed_branch, (c + 1) * vis_step))

    picked = (_top_blocks(sel_ref[...]) > 0.5) & jnp.concatenate([valid] * KV_HEADS, axis=1)
    bias_ref[0:n_selb, :] = jnp.where(picked, 0.0, NEG)
    bias_ref[n_selb:n_selb + SUBLANES, :] = jnp.full((SUBLANES, KV_HEADS * Q_BLOCK), NEG, F32)

    def sel_masks(t):
        return [t * KEY_TILE + krow <= q0 + tcol] * KV_HEADS

    flash_branch(ksel_ref, vselt_ref, (q0 + Q_BLOCK + KEY_TILE - 1) // KEY_TILE, lambda t: t * KEY_TILE, sel_masks,
                 tile_bias=lambda t: t)
    for h in range(KV_HEADS):
        tot_ref[h] = tot_ref[h] + gate(h, 1) * flash_result(h)
        qtp_ref[h, LANES:LANES + BIAS_ROWS, :] = jnp.zeros((BIAS_ROWS, nq), BF)

    n_wt = (WINDOW + Q_BLOCK + KEY_TILE - 1) // KEY_TILE
    w0 = jnp.clip(q0 - WINDOW, 0, seq - n_wt * KEY_TILE)

    def win_masks(t):
        rel = (q0 + tcol) - (w0 + t * KEY_TILE + krow)
        return [(rel >= 0) & (rel < WINDOW)] * KV_HEADS

    flash_branch(kwin_ref, vwint_ref, n_wt, lambda t: w0 + t * KEY_TILE, win_masks)

    for h in range(KV_HEADS):
        tot = tot_ref[h] + gate(h, 2) * flash_result(h)
        for g in range(Q_PER_KV):
            r = (h * Q_PER_KV + g) * HEAD_DIM
            o_ref[r:r + HEAD_DIM, :] = tot[:, g * Q_BLOCK:(g + 1) * Q_BLOCK].astype(BF)


def _nsa_prompt_call(qt, gnt, kvb, kvt, kc, vct, ovt, *, batch, seq):
    nblk = seq // Q_BLOCK
    eblk = jnp.asarray(np.arange(LANES)[None, :] == (np.arange(KEY_TILE) // SEL_BLOCK)[:, None], BF)
    n = batch * seq
    n_sub = seq // CMP_STRIDE
    n_selb = seq // SEL_BLOCK
    nq = Q_PER_KV * Q_BLOCK
    in_specs = [
        pl.BlockSpec((D_ATTN, Q_BLOCK), lambda b, i: (0, b * nblk + i)),
        pl.BlockSpec((gnt.shape[0], Q_BLOCK), lambda b, i: (0, b * nblk + i)),
        pl.BlockSpec((seq, LANES), lambda b, i: (b, 2)),
        pl.BlockSpec((seq, LANES), lambda b, i: (b, 4)),
        pl.BlockSpec((LANES, seq), lambda b, i: (3, b)),
        pl.BlockSpec((LANES, seq), lambda b, i: (5, b)),
        pl.BlockSpec((None, n_sub, LANES), lambda b, i: (b, 0, 0)),
        pl.BlockSpec((None, LANES, n_sub), lambda b, i: (b, 0, 0)),
        _full(ovt.shape),
        _full(eblk.shape),
    ]
    return pl.pallas_call(
        _nsa_prompt_kernel, grid=(batch, nblk), in_specs=in_specs,
        out_specs=pl.BlockSpec((D_ATTN, Q_BLOCK), lambda b, i: (0, b * nblk + i)),
        out_shape=jax.ShapeDtypeStruct((D_ATTN, n), BF),
        scratch_shapes=[pltpu.VMEM((n_selb, KV_HEADS * Q_BLOCK), F32),
                        pltpu.VMEM((n_selb + SUBLANES, KV_HEADS * Q_BLOCK), F32), pltpu.VMEM((KV_HEADS, 1, nq), F32),
                        pltpu.VMEM((KV_HEADS, HEAD_DIM + ONES_ROWS, nq), F32),
                        pltpu.VMEM((KV_HEADS, 2 * LANES, nq), BF),
                        pltpu.VMEM((KV_HEADS, HEAD_DIM, nq), F32),
                        pltpu.VMEM((KV_HEADS, KEY_TILE, nq), F32), pltpu.VMEM((KV_HEADS, KEY_TILE, nq), F32)],
        compiler_params=_cparams(("arbitrary", "arbitrary")), name="nsa_prompt",
    )(qt, gnt, kvb, kvb, kvt, kvt, kc, vct, ovt, eblk)


def _route(x1, wrt_ref, br_ref, tri_ref, cnt_ref):
    tm = x1.shape[0]
    logits = lax.dot_general(wrt_ref[...], x1, (((1,), (1,)), ((), ())), precision=lax.Precision.HIGHEST,
                             preferred_element_type=F32)
    aff = jax.nn.sigmoid(logits)
    grp = aff + br_ref[...]
    affr = [aff[k:k + 1, :] for k in range(N_EXPERTS)]
    grpr = [grp[k:k + 1, :] for k in range(N_EXPERTS)]
    best = None
    gsel = jnp.zeros((1, tm), jnp.int32)
    for gi in range(N_EXPERT_GROUPS):
        m = grpr[gi * 4:(gi + 1) * 4]
        top2 = None
        for a in range(4):
            for b in range(a + 1, 4):
                s = m[a] + m[b]
                top2 = s if top2 is None else jnp.maximum(top2, s)
        if best is None:
            best = top2
        else:
            better = top2 > best
            gsel = jnp.where(better, gi, gsel)
            best = jnp.where(better, top2, best)
    ing, ina = [], []
    for j in range(4):
        vg, va = grpr[j], affr[j]
        for gi in range(1, N_EXPERT_GROUPS):
            vg = jnp.where(gsel == gi, grpr[gi * 4 + j], vg)
            va = jnp.where(gsel == gi, affr[gi * 4 + j], va)
        ing.append(vg)
        ina.append(va)

    def argmax4(vals):
        bv, bi = vals[0], jnp.zeros((1, tm), jnp.int32)
        for j in range(1, 4):
            better = vals[j] > bv
            bi = jnp.where(better, j, bi)
            bv = jnp.where(better, vals[j], bv)
        return bi

    loc1 = argmax4(ing)
    loc2 = argmax4([jnp.where(loc1 == j, -3e38, ing[j]) for j in range(4)])
    w1 = sum(jnp.where(loc1 == j, ina[j], 0.0) for j in range(4))
    w2 = sum(jnp.where(loc2 == j, ina[j], 0.0) for j in range(4))
    tot = w1 + w2
    wd = [jnp.where(loc1 == j, w1 / tot, 0.0) + jnp.where(loc2 == j, w2 / tot, 0.0) for j in range(4)]
    oh = jnp.concatenate([(gsel == gi).astype(F32) for gi in range(N_EXPERT_GROUPS)]
                         + [jnp.zeros((SUBLANES - N_EXPERT_GROUPS, tm), F32)], axis=0)
    cum = _dot(oh.astype(BF), tri_ref[...])
    carry = cnt_ref[...][:, 0:1]
    rank = jnp.sum(oh * (cum + carry), axis=0, keepdims=True)
    cnt_ref[...] = cnt_ref[...] + jnp.sum(oh, axis=1, keepdims=True)
    rows = wd + [gsel.astype(F32), rank, jnp.zeros((ROUTE_W - 6, tm), F32)]
    return jnp.concatenate(rows, axis=0)


def _merge_kernel(x_ref, ya_ref, sgb_ref, o_ref, wpb_ref, wo_ref, g_ref, b_ref, wrt_ref, br_ref, tri_ref,
                  xr_ref, cnt_out_ref, cnt_ref, *, alpha, o_transposed):
    @pl.when(pl.program_id(0) == 0)
    def _():
        cnt_ref[...] = jnp.zeros(cnt_ref.shape, F32)

    if o_transposed:
        ob = _dot_tn(o_ref[...], wpb_ref[...])
    else:
        ob = _dot(o_ref[...].astype(BF), wpb_ref[...])
    y = ya_ref[...] + sgb_ref[...] * ob
    mix = _dot(y.astype(BF), wo_ref[...])
    x1 = _layer_norm(alpha * x_ref[...] + mix, g_ref[...], b_ref[...])
    xr_ref[:, 0:D_MODEL] = x1
    info = _route(x1, wrt_ref, br_ref, tri_ref, cnt_ref)
    xr_ref[:, D_MODEL:XR_W] = info.T
    cnt_out_ref[...] = cnt_ref[...]


def _merge_call(x, ya, sgb, o, wpb, wo, g, b, wrt, br, tri, *, tm, alpha, o_transposed):
    n = ya.shape[0]
    tok = pl.BlockSpec((tm, D_MODEL), lambda i: (i, 0))
    ospec = pl.BlockSpec((D_ATTN, tm), lambda i: (0, i)) if o_transposed else tok
    return pl.pallas_call(
        functools.partial(_merge_kernel, alpha=alpha, o_transposed=o_transposed), grid=(n // tm,),
        in_specs=[tok, tok, tok, ospec, _full(wpb.shape), _full(wo.shape), _full(g.shape), _full(b.shape),
                  _full(wrt.shape), _full(br.shape), _full(tri.shape)],
        out_specs=[pl.BlockSpec((tm, XR_W), lambda i: (i, 0)), _full((SUBLANES, LANES))],
        out_shape=[jax.ShapeDtypeStruct((n, XR_W), F32), jax.ShapeDtypeStruct((SUBLANES, LANES), F32)],
        scratch_shapes=[pltpu.VMEM((SUBLANES, LANES), F32)],
        compiler_params=_cparams(("arbitrary",)), name="merge_t" if o_transposed else "merge_n",
    )(x, ya, sgb, o, wpb, wo, g, b, wrt, br, tri)


def _row_copy(src_ref, dst_ref, sem, src_row, dst_row):
    return pltpu.make_async_copy(src_ref.at[pl.ds(src_row, 1), :], dst_ref.at[pl.ds(dst_row, 1), :], sem)


def _scatter_kernel(dest_ref, xr_ref, xs_in_ref, xs_ref, sem):
    del xs_in_ref
    tm = xr_ref.shape[0]

    def start(grp, c):
        for u in range(ROW_DMA_UNROLL):
            r = grp * ROW_DMA_UNROLL + u
            _row_copy(xr_ref, xs_ref, sem, r, dest_ref[r]).start(priority=u % 2)
        return c

    def wait(r, c):
        _row_copy(xr_ref, xs_ref, sem, r, dest_ref[r]).wait()
        return c

    lax.fori_loop(0, tm // ROW_DMA_UNROLL, start, 0)
    lax.fori_loop(0, tm, wait, 0, unroll=ROW_DMA_UNROLL)


def _scatter_call(dest, xr, xs0, *, tm):
    n = xr.shape[0]
    return pl.pallas_call(
        _scatter_kernel, grid=(n // tm,),
        in_specs=[pl.BlockSpec((tm,), lambda i: (i,), memory_space=pltpu.SMEM),
                  pl.BlockSpec((tm, XR_W), lambda i: (i, 0)), pl.BlockSpec(memory_space=pl.ANY)],
        out_specs=pl.BlockSpec(memory_space=pl.ANY),
        out_shape=jax.ShapeDtypeStruct(xs0.shape, F32),
        scratch_shapes=[pltpu.SemaphoreType.DMA(())],
        input_output_aliases={2: 0},
        compiler_params=_cparams(("arbitrary",)), name="moe_scatter",
    )(dest, xr, xs0)


def _experts_kernel(tg_ref, nu_ref, xs_ref, wg_ref, wu_ref, wd_ref, ys_ref):
    del tg_ref
    j = pl.program_id(0)

    @pl.when(j < nu_ref[0])
    def _():
        xb = xs_ref[:, 0:D_MODEL].astype(BF)
        acc = None
        for e in range(EXPERTS_PER_GROUP):
            hidden = jax.nn.silu(_dot(xb, wg_ref[e])) * _dot(xb, wu_ref[e])
            hidden = hidden * xs_ref[:, D_MODEL + e:D_MODEL + e + 1]
            y = _dot(hidden.astype(BF), wd_ref[e])
            acc = y if acc is None else acc + y
        ys_ref[...] = acc

    @pl.when(j >= nu_ref[0])
    def _():
        ys_ref[...] = jnp.zeros(ys_ref.shape, F32)


def _experts_call(tile_group, n_used, xs, wg, wu, wd, *, tmx):
    rows = xs.shape[0]
    wspec_in = pl.BlockSpec((None, EXPERTS_PER_GROUP, D_MODEL, D_EXPERT), lambda j, tg, nu: (tg[j], 0, 0, 0))
    wspec_out = pl.BlockSpec((None, EXPERTS_PER_GROUP, D_EXPERT, D_MODEL), lambda j, tg, nu: (tg[j], 0, 0, 0))
    grid_spec = pltpu.PrefetchScalarGridSpec(
        num_scalar_prefetch=2, grid=(rows // tmx,),
        in_specs=[pl.BlockSpec((tmx, XR_W), lambda j, tg, nu: (j, 0)), wspec_in, wspec_in, wspec_out],
        out_specs=pl.BlockSpec((tmx, D_MODEL), lambda j, tg, nu: (j, 0)))
    return pl.pallas_call(
        _experts_kernel, grid_spec=grid_spec, out_shape=jax.ShapeDtypeStruct((rows, D_MODEL), F32),
        compiler_params=_cparams(("arbitrary",)), name="moe_experts",
    )(tile_group, n_used, xs, wg, wu, wd)


def _combine_kernel(dest_ref, xr_ref, ys_ref, g_ref, b_ref, x2_ref, buf_ref, sem, *, alpha):
    tm = xr_ref.shape[0]

    def start(grp, c):
        for u in range(ROW_DMA_UNROLL):
            r = grp * ROW_DMA_UNROLL + u
            _row_copy(ys_ref, buf_ref, sem, dest_ref[r], r).start(priority=u % 2)
        return c

    def wait(r, c):
        _row_copy(ys_ref, buf_ref, sem, dest_ref[r], r).wait()
        return c

    lax.fori_loop(0, tm // ROW_DMA_UNROLL, start, 0)
    lax.fori_loop(0, tm, wait, 0, unroll=ROW_DMA_UNROLL)
    x2_ref[...] = _layer_norm(alpha * xr_ref[...] + buf_ref[...], g_ref[...], b_ref[...])


def _combine_call(dest, xr, ys, g, b, *, tm, alpha):
    n = xr.shape[0]
    return pl.pallas_call(
        functools.partial(_combine_kernel, alpha=alpha), grid=(n // tm,),
        in_specs=[pl.BlockSpec((tm,), lambda i: (i,), memory_space=pltpu.SMEM),
                  pl.BlockSpec((tm, D_MODEL), lambda i: (i, 0)), pl.BlockSpec(memory_space=pl.ANY),
                  _full(g.shape), _full(b.shape)],
        out_specs=pl.BlockSpec((tm, D_MODEL), lambda i: (i, 0)),
        out_shape=jax.ShapeDtypeStruct((n, D_MODEL), F32),
        scratch_shapes=[pltpu.VMEM((tm, D_MODEL), F32), pltpu.SemaphoreType.DMA(())],
        compiler_params=_cparams(("arbitrary",)), name="moe_combine",
    )(dest, xr, ys, g, b)


def _moe(xr, cnt, lw, *, tm, tmx, alpha):
    n = xr.shape[0]
    n_tiles = n // tmx + N_EXPERT_GROUPS
    counts = cnt[:N_EXPERT_GROUPS, 0].astype(jnp.int32)
    tiles_per = (counts + tmx - 1) // tmx
    tile_end = jnp.cumsum(tiles_per)
    offs = (tile_end - tiles_per) * tmx
    gid = xr[:, D_MODEL + 4].astype(jnp.int32)
    rank = xr[:, D_MODEL + 5].astype(jnp.int32)
    dest = offs[gid] + rank
    tile_group = jnp.minimum(jnp.sum(jnp.arange(n_tiles)[:, None] >= tile_end[None, :], axis=1),
                             N_EXPERT_GROUPS - 1).astype(jnp.int32)
    n_used = tile_end[-1:].astype(jnp.int32)
    xs = _scatter_call(dest, xr, jnp.zeros((n_tiles * tmx, XR_W), F32), tm=tm)
    ys = _experts_call(tile_group, n_used, xs, lw["wg"], lw["wu_e"], lw["wd"], tmx=tmx)
    return _combine_call(dest, xr, ys, lw["ln2_g"], lw["ln2_b"], tm=tm, alpha=alpha)


def _softmax_with_new(s, mask, s_new, new_on):
    s = jnp.where(mask, s, NEG)
    s_new = jnp.where(new_on, s_new, NEG)
    mx = jnp.maximum(jnp.max(s, axis=1, keepdims=True), s_new)
    p = jnp.where(mask, jnp.exp(s - mx), 0.0)
    p_new = jnp.where(new_on, jnp.exp(s_new - mx), 0.0)
    den = jnp.maximum(jnp.sum(p, axis=1, keepdims=True) + p_new, 1e-30)
    return p, p_new, 1.0 / den


def _sample_attention(qs, news, gates, cw_refs, x_refs, sel_refs, wp_ref, w2p_ref, w2pt_ref, ppos_ref, ov_ref,
                      exp_ref, *, n_past):
    n_seq = len(qs)
    seqs = range(n_seq)
    n_sub = n_past // CMP_STRIDE
    n_cmp = n_sub - CMP_BLOCK // CMP_STRIDE + 1
    nrow = KV_HEADS * Q_PER_KV
    qpos = n_past
    qfs = [q.astype(F32) for q in qs]

    def new_score(k, col):
        kn = news[k][:, col * LANES:(col + 1) * LANES].astype(BF).astype(F32)
        return jnp.sum(qfs[k] * kn, axis=1, keepdims=True)

    def new_value(k, col):
        return news[k][:, col * LANES:(col + 1) * LANES].astype(BF).astype(F32)

    def rows(x, idx, n):
        return x[idx * n:(idx + 1) * n]

    g_all = []
    for slot in range(2):
        w = wp_ref[slot]
        pp = _dot(ppos_ref[slot], w)
        pos = pp[0:1, 0:2 * CMP_HIDDEN] + pp[1:2, 2 * CMP_HIDDEN:]
        x_all = jnp.concatenate(
            [jnp.concatenate([x_refs[k][slot, pl.ds(s, n_sub, stride=CMP_STRIDE), :].astype(BF)
                              for s in range(CMP_STRIDE)], axis=1) for k in seqs], axis=0)
        hh_all = _dot(x_all, w)
        g_seq = []
        for k in seqs:
            hh = rows(hh_all, k, n_sub)
            h1 = pltpu.roll(hh[:, 2 * CMP_HIDDEN:], n_sub - 1, 0)
            g_seq.append(jax.nn.gelu(hh[:, 0:2 * CMP_HIDDEN] + h1 + pos).astype(BF))
        g_all.append(jnp.concatenate(g_seq, axis=0))
    kc_all = _dot(g_all[0], w2p_ref[0]).astype(BF)
    vct_all = _dot_nt(w2pt_ref[1], g_all[1]).astype(BF)

    n_idx = lax.broadcasted_iota(jnp.int32, (nrow, n_sub), 1)
    vis = (n_idx * CMP_STRIDE + (CMP_BLOCK - 1) <= qpos) & (n_idx < n_cmp)
    s_cs = [_dot_nt(qs[k], rows(kc_all, k, n_sub)) for k in seqs]
    p_cs = []
    for k in seqs:
        s_c = jnp.where(vis, s_cs[k], NEG)
        mx = jnp.max(s_c, axis=1, keepdims=True)
        p_c = jnp.where(vis, jnp.exp(s_c - mx), 0.0)
        p_cs.append(p_c * (1.0 / jnp.maximum(jnp.sum(p_c, axis=1, keepdims=True), 1e-30)))
    o_cs = [_dot_nt(p_cs[k].astype(BF), vct_all[:, k * n_sub:(k + 1) * n_sub]) for k in seqs]

    psum = jnp.concatenate([jnp.sum(p_cs[k][h * Q_PER_KV:(h + 1) * Q_PER_KV], axis=0, keepdims=True)
                            for k in seqs for h in range(KV_HEADS)], axis=0)
    imp_rows = -(-psum.shape[0] // SUBLANES) * SUBLANES
    if imp_rows > psum.shape[0]:
        psum = jnp.concatenate([psum, jnp.zeros((imp_rows - psum.shape[0], n_sub), F32)], axis=0)
    p_hi = psum.astype(BF)
    p_lo = (psum - p_hi.astype(F32)).astype(BF)
    imp = _dot(p_hi, ov_ref[...]) + _dot(p_lo, ov_ref[...])
    n_selp = imp.shape[1]
    j_idx = lax.broadcasted_iota(jnp.int32, (imp_rows, n_selp), 1)
    cur = qpos // SEL_BLOCK
    forced = (j_idx == 0) | (j_idx == cur) | (j_idx == cur - 1)
    valid = j_idx * SEL_BLOCK <= qpos
    score = jnp.where(valid, imp + jnp.where(forced, FORCE_BONUS, 0.0), NEG)
    score_t = jnp.concatenate([score, jnp.zeros((n_selp - imp_rows, n_selp), F32)], axis=0).T
    jr = lax.broadcasted_iota(jnp.int32, (n_selp, n_selp), 0)
    jc = lax.broadcasted_iota(jnp.int32, (n_selp, n_selp), 1)
    sel_rows = []
    for r in range(n_seq * KV_HEADS):
        other = jnp.broadcast_to(score_t[:, r:r + 1], (n_selp, n_selp))
        mine = jnp.broadcast_to(score[r:r + 1, :], (n_selp, n_selp))
        ahead = (other > mine) | ((other == mine) & (jr < jc))
        rank = jnp.sum(ahead.astype(F32), axis=0, keepdims=True)
        sel_rows.append(jnp.broadcast_to((rank < N_SELECT).astype(F32), (Q_PER_KV, n_selp)))
    sel = jnp.concatenate(sel_rows, axis=0)

    on_all = _dot(sel.astype(BF), exp_ref[...]) > 0.5
    s_ss = [_dot_nt(qs[k], sel_refs[k][:, 0:LANES].astype(BF)) for k in seqs]
    soft = [_softmax_with_new(s_ss[k], rows(on_all, k, nrow), new_score(k, 2), rows(sel, k, nrow)[:, cur:cur + 1] > 0.5)
            for k in seqs]
    pv_s = [_dot(soft[k][0].astype(BF), sel_refs[k][:, LANES:2 * LANES].astype(BF)) for k in seqs]
    o_ss = [(pv_s[k] + soft[k][1] * new_value(k, 3)) * soft[k][2] for k in seqs]

    wb = cw_refs[0].shape[0]
    wpos = (n_past - wb) + lax.broadcasted_iota(jnp.int32, (nrow, wb), 1)
    rel = qpos - wpos
    wmask = (rel >= 0) & (rel < WINDOW) & (wpos >= 0)
    s_ws = [_dot_nt(qs[k], cw_refs[k][:, 0:LANES]) for k in seqs]
    soft = [_softmax_with_new(s_ws[k], wmask, new_score(k, 4), jnp.full((nrow, 1), True)) for k in seqs]
    pv_w = [_dot(soft[k][0].astype(BF), cw_refs[k][:, LANES:2 * LANES]) for k in seqs]
    o_ws = [(pv_w[k] + soft[k][1] * new_value(k, 5)) * soft[k][2] for k in seqs]

    return [gates[k][:, 0:1] * o_cs[k] + gates[k][:, 1:2] * o_ss[k] + gates[k][:, 2:3] * o_ws[k] for k in seqs]


SEQ_PER_STEP = 4


def _nsa_sample_kernel(pt_ref, q_ref, new_ref, gate_ref, cw_ref, cache_ref, wp_ref, w2p_ref, w2pt_ref,
                       ppos_ref, ov_ref, exp_ref, o_ref, x_buf, sel_buf, sems, *, layer, n_pages, n_past):
    j = pl.program_id(0)
    n_steps = pl.num_programs(0)
    cur = lax.rem(j, 2)

    def step_copies(buf_set, step):
        copies = []
        for k in range(SEQ_PER_STEP):
            for p in range(n_pages):
                page = cache_ref.at[pt_ref[(step * SEQ_PER_STEP + k) * n_pages + p], layer]
                dst_rows = pl.ds(p * PAGE_SIZE, PAGE_SIZE)
                for slot in range(2):
                    copies.append(pltpu.make_async_copy(page.at[:, pl.ds(slot * LANES, LANES)],
                                                        x_buf.at[buf_set, k, slot, dst_rows, :], sems.at[buf_set]))
                copies.append(pltpu.make_async_copy(page.at[:, pl.ds(2 * LANES, 2 * LANES)],
                                                    sel_buf.at[buf_set, k, dst_rows, :], sems.at[buf_set]))
        return copies

    @pl.when(j == 0)
    def _():
        for c in step_copies(cur, j):
            c.start()

    @pl.when(j + 1 < n_steps)
    def _():
        for c in step_copies(1 - cur, j + 1):
            c.start()

    for c in step_copies(cur, j):
        c.wait()
    ks = range(SEQ_PER_STEP)
    outs = _sample_attention([q_ref[k] for k in ks], [new_ref[k] for k in ks], [gate_ref[k] for k in ks],
                             [cw_ref.at[k] for k in ks], [x_buf.at[cur, k] for k in ks],
                             [sel_buf.at[cur, k] for k in ks], wp_ref, w2p_ref, w2pt_ref, ppos_ref, ov_ref, exp_ref,
                             n_past=n_past)
    for k in ks:
        o_ref[k] = outs[k]


def _nsa_sample_call(pt, qp, new, gates, cwin, cache, cw, ov, expand, *, layer, n_past):
    n_dec = qp.shape[0]
    n_pages = n_past // PAGE_SIZE
    nrow = KV_HEADS * Q_PER_KV
    wb = cwin.shape[2]
    wp, w2p, w2pt, ppos = cw
    const = lambda shape: pl.BlockSpec(shape, lambda b, pt_: (0,) * len(shape))
    per_seq = lambda *tail: pl.BlockSpec((SEQ_PER_STEP,) + tail, lambda b, pt_: (b,) + (0,) * len(tail))
    grid_spec = pltpu.PrefetchScalarGridSpec(
        num_scalar_prefetch=1, grid=(n_dec // SEQ_PER_STEP,),
        in_specs=[per_seq(nrow, LANES), per_seq(1, new.shape[2]), per_seq(nrow, LANES),
                  pl.BlockSpec((SEQ_PER_STEP, None, wb, 2 * LANES), lambda b, pt_: (b, layer, 0, 0)),
                  pl.BlockSpec(memory_space=pl.ANY),
                  const(wp.shape), const(w2p.shape), const(w2pt.shape), const(ppos.shape), const(ov.shape),
                  const(expand.shape)],
        out_specs=per_seq(nrow, LANES),
        scratch_shapes=[pltpu.VMEM((2, SEQ_PER_STEP, 2, n_past, LANES), F32),
                        pltpu.VMEM((2, SEQ_PER_STEP, n_past, 2 * LANES), F32),
                        pltpu.SemaphoreType.DMA((2,))])
    return pl.pallas_call(
        functools.partial(_nsa_sample_kernel, layer=layer, n_pages=n_pages, n_past=n_past),
        grid_spec=grid_spec, out_shape=jax.ShapeDtypeStruct((n_dec, nrow, LANES), F32),
        compiler_params=_cparams(("arbitrary",)), name="nsa_sample",
    )(pt, qp, new, gates, cwin, cache, wp, w2p, w2pt, ppos, ov, expand)


def _overlap(n_cmp, n_sel):
    c0 = np.arange(n_cmp) * CMP_STRIDE
    s0 = np.arange(n_sel) * SEL_BLOCK
    return ((c0[:, None] < s0[None, :] + SEL_BLOCK) & (c0[:, None] + CMP_BLOCK > s0[None, :])).astype(np.float32)


def _layer_weights(l, w_in, gmlp_ln_g, gmlp_ln_b, gmlp_ws, gmlp_bs, cmp_pos, cmp_w1, cmp_w2, w_branch_a, w_branch_b,
                   w_out, ln1_g, ln1_b, ln2_g, ln2_b, w_gate, w_up, w_down):
    cuts = np.cumsum((0,) + PROJ_SIZES)
    seg = lambda i, j=None: w_in[l][:, cuts[i]:cuts[i + 1 if j is None else j]].astype(BF)
    wgn = seg(9)
    eye = jnp.eye(KV_HEADS, dtype=F32)
    r = CMP_BLOCK // CMP_STRIDE
    w1r = cmp_w1[l].reshape(2, r, CMP_STRIDE, HEAD_DIM, CMP_HIDDEN)
    wp = jnp.einsum("xrsdf,hk->xshdrkf", w1r, eye).reshape(2, CMP_STRIDE * D_KV, r * KV_HEADS * CMP_HIDDEN)
    w2p = jnp.einsum("xfd,hk->xhfkd", cmp_w2[l], eye).reshape(2, KV_HEADS * CMP_HIDDEN, D_KV)
    pos = cmp_pos[l].reshape(2, r, CMP_STRIDE, 1, HEAD_DIM)
    ppos = jnp.broadcast_to(pos, (2, r, CMP_STRIDE, KV_HEADS, HEAD_DIM)).reshape(2, r, CMP_STRIDE * D_KV)
    ppos = jnp.pad(ppos, ((0, 0), (0, SUBLANES - r), (0, 0)))
    ex = lambda w: w.reshape((N_EXPERT_GROUPS, EXPERTS_PER_GROUP) + w.shape[1:]).astype(BF)
    row = lambda v: v[l][None, :].astype(F32)
    return {
        "wu": seg(0), "wv": seg(1), "wq": seg(2), "wkv": seg(3, 9), "wga": seg(10), "wgb": seg(11),
        "wqt": seg(2).T, "wkvt": seg(3, 9).T,
        "wgnt": wgn.reshape(D_MODEL, KV_HEADS, Q_PER_KV, 3).transpose(1, 3, 2, 0).reshape(3 * N_HEADS, D_MODEL),
        "wgn": jnp.pad(wgn, ((0, 0), (0, LANES - 3 * N_HEADS))),
        "gln_g": row(gmlp_ln_g), "gln_b": row(gmlp_ln_b),
        "ws": gmlp_ws[l], "bs_t": gmlp_bs[l].T,
        "ws0": jnp.repeat(gmlp_ws[l][:, 0, 0], CHUNK)[None, :], "bs0": jnp.repeat(gmlp_bs[l][:, 0], CHUNK)[None, :],
        "cw": (wp.astype(BF), w2p.astype(BF), w2p.transpose(0, 2, 1).astype(BF), ppos.astype(BF)),
        "wpa": w_branch_a[l].astype(BF), "wpb": w_branch_b[l].astype(BF), "wo": w_out[l].astype(BF),
        "ln1_g": row(ln1_g), "ln1_b": row(ln1_b), "ln2_g": row(ln2_g), "ln2_b": row(ln2_b),
        "wg": ex(w_gate[l]), "wu_e": ex(w_up[l]), "wd": ex(w_down[l]),
    }


def _strict_upper(n):
    return jnp.asarray(np.triu(np.ones((n, n), np.float32), 1), BF)


def kernel(x_prompt, x_sample, cache_kv, cache_win, page_table, w_in, gmlp_ln_g, gmlp_ln_b, gmlp_ws, gmlp_bs,
           cmp_pos, cmp_w1, cmp_w2, w_branch_a, w_branch_b, w_out, ln1_g, ln1_b, ln2_g, ln2_b,
           w_router, b_router, w_gate, w_up, w_down):
    depth = w_in.shape[0]
    alpha = (2.0 * depth) ** 0.25
    batch, seq, _ = x_prompt.shape
    n_dec, t_dec = x_sample.shape[:2]
    n_past = page_table.shape[1] * PAGE_SIZE
    assert t_dec == 1 and seq % (4 * CHUNK) == 0 and n_past % PAGE_SIZE == 0 and n_dec % SUBLANES == 0
    tm_p, tmx_p = 4 * CHUNK, 2 * CHUNK
    tm_s = tmx_s = n_dec
    wb = cache_win.shape[2]

    n_sub_p = seq // CMP_STRIDE
    ovt_p = jnp.asarray(np.pad(_overlap(n_sub_p - 1, seq // SEL_BLOCK), ((0, 1), (0, 0))).T, BF)
    n_sub_s = n_past // CMP_STRIDE
    n_sel_s = n_past // SEL_BLOCK + 1
    ov_s = jnp.asarray(np.pad(_overlap(n_sub_s - 1, n_sel_s), ((0, 1), (0, LANES - n_sel_s))), BF)
    expand = jnp.asarray(np.arange(LANES)[:, None] == (np.arange(n_past) // SEL_BLOCK)[None, :], BF)
    wrt = w_router.T.astype(F32)
    br = b_router[:, None].astype(F32)
    tri_p, tri_s = _strict_upper(tm_p), _strict_upper(tm_s)
    cache = cache_kv.reshape(cache_kv.shape[0], depth, PAGE_SIZE, N_KV_SLOTS * D_KV)
    cwin = cache_win.reshape(n_dec, depth, wb, 2 * D_KV).astype(BF)
    pt = page_table.reshape(-1).astype(jnp.int32)

    xp = x_prompt.reshape(batch * seq, D_MODEL)
    xs = x_sample.reshape(n_dec, D_MODEL)
    kv_p, kv_s, win_s, gv_s = [], [], [], []
    for l in range(depth):
        lw = _layer_weights(l, w_in, gmlp_ln_g, gmlp_ln_b, gmlp_ws, gmlp_bs, cmp_pos, cmp_w1, cmp_w2, w_branch_a,
                            w_branch_b, w_out, ln1_g, ln1_b, ln2_g, ln2_b, w_gate, w_up, w_down)
        (ya,) = _gmlp_call(xp, lw["wu"], lw["wv"], lw["wga"], lw["wpa"], lw["gln_g"], lw["gln_b"], lw["ws"],
                           lw["bs_t"], tm=tm_p, sample=False)
        sgb, kv, kvb, kvt, qt, gnt = _attn_in_prompt_call(xp, lw["wgb"], lw["wkv"], lw["wqt"], lw["wkvt"], lw["wgnt"],
                                                          tm=tm_p)
        kc, vct = _compress_call(kv, lw["cw"], batch=batch, seq=seq)
        ot = _nsa_prompt_call(qt, gnt, kvb, kvt, kc, vct, ovt_p, batch=batch, seq=seq)
        xr, cnt = _merge_call(xp, ya, sgb, ot, lw["wpb"], lw["wo"], lw["ln1_g"], lw["ln1_b"], wrt, br, tri_p,
                              tm=tm_p, alpha=alpha, o_transposed=True)
        xp = _moe(xr, cnt, lw, tm=tm_p, tmx=tmx_p, alpha=alpha)
        kv_p.append(kv)
        ya, vn = _gmlp_call(xs, lw["wu"], lw["wv"], lw["wga"], lw["wpa"], lw["gln_g"], lw["gln_b"], lw["ws0"],
                            lw["bs0"], tm=tm_s, sample=True)
        sgb, kv, q, gn = _attn_in_sample_call(xs, lw["wgb"], lw["wkv"], lw["wq"], lw["wgn"])
        qh = q.reshape(n_dec, KV_HEADS, Q_PER_KV, HEAD_DIM)
        zq = jnp.zeros_like(qh[:, 0])
        qp = jnp.concatenate([jnp.concatenate([qh[:, 0], zq], -1), jnp.concatenate([zq, qh[:, 1]], -1)], 1).astype(BF)
        gates = jnp.pad(gn[:, :3 * N_HEADS].reshape(n_dec, N_HEADS, 3), ((0, 0), (0, 0), (0, LANES - 3)))
        o16 = _nsa_sample_call(pt, qp, kv[:, None, :], gates, cwin, cache, lw["cw"], ov_s, expand, layer=l,
                               n_past=n_past)
        o = jnp.concatenate([o16[:, :Q_PER_KV, :HEAD_DIM], o16[:, Q_PER_KV:, HEAD_DIM:]], 1).reshape(n_dec, D_ATTN)
        xr, cnt = _merge_call(xs, ya, sgb, o, lw["wpb"], lw["wo"], lw["ln1_g"], lw["ln1_b"], wrt, br, tri_s,
                              tm=tm_s, alpha=alpha, o_transposed=False)
        xs = _moe(xr, cnt, lw, tm=tm_s, tmx=tmx_s, alpha=alpha)
        kvr = kv.reshape(n_dec, 1, 6, KV_HEADS, HEAD_DIM)
        kv_s.append(kvr[:, :, :N_KV_SLOTS])
        win_s.append(jnp.concatenate([cache_win[:, l], kvr[:, :, N_KV_SLOTS:]], axis=1)[:, -wb:])
        gv_s.append(vn[:, None, :])
    kv_all = jnp.stack(kv_p, axis=0).reshape(depth, batch, seq, 6, KV_HEADS, HEAD_DIM)
    kv_rows_prompt = kv_all[:, :, :, :N_KV_SLOTS].transpose(1, 0, 2, 3, 4, 5)
    win_prompt = kv_all[:, :, seq - min(WINDOW, seq):, N_KV_SLOTS:].transpose(1, 0, 2, 3, 4, 5)
    return (xp.reshape(batch, seq, D_MODEL), xs.reshape(n_dec, 1, D_MODEL), kv_rows_prompt, win_prompt,
            jnp.stack(kv_s, axis=1), jnp.stack(win_s, axis=1), jnp.stack(gv_s, axis=1))
```

```python
import functools

import numpy as np
import jax
import jax.numpy as jnp
from jax import lax
from jax.experimental import pallas as pl
from jax.experimental.pallas import tpu as pltpu

D_MODEL = 1024
CHUNK = 128
D_GMLP = 1024
GMLP_GROUPS = 8
N_HEADS = 16
KV_HEADS = 2
Q_PER_KV = N_HEADS // KV_HEADS
HEAD_DIM = 64
D_ATTN = N_HEADS * HEAD_DIM
D_KV = KV_HEADS * HEAD_DIM
CMP_BLOCK = 32
CMP_STRIDE = 16
CMP_HIDDEN = 2 * HEAD_DIM
SEL_BLOCK = 64
N_SELECT = 16
WINDOW = 512
Q_BLOCK = 128
FORCE_BONUS = 1e4
N_KV_SLOTS = 4
N_EXPERTS = 16
N_EXPERT_GROUPS = 4
EXPERTS_PER_GROUP = 4
D_EXPERT = 512
PAGE_SIZE = 128
PROJ_SIZES = (D_GMLP, D_GMLP, D_ATTN) + (D_KV,) * 6 + (3 * N_HEADS, D_MODEL, D_MODEL)

LANES = 128
SUBLANES = 8
VMEM_LIMIT_BYTES = 56 * 1024 * 1024

NEG = -1e30
LOG2E = 1.4426950408889634
KEY_TILE = 4 * Q_BLOCK
COL_CHUNK = 2 * LANES
CMP_VIS_ROWS = 128
BIAS_ROWS = 16
ONES_ROWS = 16
BF = jnp.bfloat16
F32 = jnp.float32
ROUTE_W = LANES
XR_W = D_MODEL + ROUTE_W
ROW_DMA_UNROLL = 8


def _cparams(sem):
    return pltpu.CompilerParams(dimension_semantics=sem, vmem_limit_bytes=VMEM_LIMIT_BYTES)


def _layer_norm(x, g, b, eps=1e-5):
    mu = jnp.mean(x, -1, keepdims=True)
    xc = x - mu
    var = jnp.mean(xc * xc, -1, keepdims=True)
    return xc * lax.rsqrt(var + eps) * g + b


def _dot(a, b):
    return jnp.dot(a, b, preferred_element_type=F32)


def _dot_nt(a, b):
    return lax.dot_general(a, b, (((1,), (1,)), ((), ())), preferred_element_type=F32)


def _dot_tn(a, b):
    return lax.dot_general(a, b, (((0,), (0,)), ((), ())), preferred_element_type=F32)


def _full(shape):
    nd = len(shape)
    return pl.BlockSpec(shape, lambda *_: (0,) * nd)


def _gmlp_kernel(x_ref, wu_ref, wv_ref, wga_ref, wpa_ref, lng_ref, lnb_ref, ws_ref, bs_ref,
                 ya_ref, *rest, sample):
    xb = x_ref[...].astype(BF)
    u = jax.nn.gelu(_dot(xb, wu_ref[...]))
    v = jax.nn.gelu(_dot(xb, wv_ref[...]))
    vn = _layer_norm(v, lng_ref[...], lnb_ref[...])
    tm = xb.shape[0]
    if sample:
        a = u * (vn * ws_ref[...] + bs_ref[...])
        rest[0][...] = vn
    else:
        n_chunk = tm // CHUNK
        vb = vn.astype(BF)
        row = lax.broadcasted_iota(jnp.int32, (CHUNK, CHUNK), 0)
        col = lax.broadcasted_iota(jnp.int32, (CHUNK, CHUNK), 1)
        causal = col <= row
        bs = bs_ref[...]
        cols = []
        for g in range(GMLP_GROUPS):
            lo = g * CHUNK
            wg = jnp.where(causal, ws_ref[g], 0.0).astype(BF)
            vg = jnp.concatenate([vb[c * CHUNK:(c + 1) * CHUNK, lo:lo + CHUNK] for c in range(n_chunk)], axis=1)
            mixed = _dot(wg, vg) + bs[:, g:g + 1]
            cols.append(jnp.concatenate([mixed[:, c * CHUNK:(c + 1) * CHUNK] for c in range(n_chunk)], axis=0))
        a = u * jnp.concatenate(cols, axis=1)
    ga = jax.nn.sigmoid(_dot(xb, wga_ref[...]))
    ya_ref[...] = ga * _dot(a.astype(BF), wpa_ref[...])


def _gmlp_call(x, wu, wv, wga, wpa, lng, lnb, ws, bs, *, tm, sample):
    n = x.shape[0]
    wspec = _full((D_MODEL, D_MODEL))
    vspec = _full((1, D_MODEL))
    in_specs = [pl.BlockSpec((tm, D_MODEL), lambda i: (i, 0)), wspec, wspec, wspec, wspec, vspec, vspec,
                _full(ws.shape), _full(bs.shape)]
    tok = pl.BlockSpec((tm, D_MODEL), lambda i: (i, 0))
    out_shape = [jax.ShapeDtypeStruct((n, D_MODEL), F32)]
    out_specs = [tok]
    if sample:
        out_shape.append(jax.ShapeDtypeStruct((n, D_MODEL), F32))
        out_specs.append(tok)
    return pl.pallas_call(
        functools.partial(_gmlp_kernel, sample=sample),
        grid=(n // tm,), in_specs=in_specs, out_specs=out_specs, out_shape=out_shape,
        compiler_params=_cparams(("arbitrary",)), name="gmlp_sample" if sample else "gmlp_prompt",
    )(x, wu, wv, wga, wpa, lng, lnb, ws, bs)


def _attn_in_prompt_kernel(x_ref, wgb_ref, wkv_ref, wqt_ref, wkvt_ref, wgnt_ref,
                           sgb_ref, kv_ref, kvb_ref, kvt_ref, qt_ref, gnt_ref):
    xb = x_ref[...].astype(BF)
    sgb_ref[...] = jax.nn.sigmoid(_dot(xb, wgb_ref[...]))
    kv = _dot(xb, wkv_ref[...])
    kv_ref[...] = kv
    kvb_ref[...] = kv.astype(BF)
    kvt_ref[...] = _dot_nt(wkvt_ref[...], xb).astype(BF)
    qt_ref[...] = (_dot_nt(wqt_ref[...], xb) * (LOG2E * HEAD_DIM ** -0.5)).astype(BF)
    gnt_ref[...] = jax.nn.sigmoid(_dot_nt(wgnt_ref[...], xb))


def _attn_in_prompt_call(x, wgb, wkv, wqt, wkvt, wgnt, *, tm):
    n = x.shape[0]
    nkv = wkv.shape[1]
    ngn = wgnt.shape[0]
    in_specs = [pl.BlockSpec((tm, D_MODEL), lambda i: (i, 0)), _full(wgb.shape), _full(wkv.shape),
                _full(wqt.shape), _full(wkvt.shape), _full(wgnt.shape)]
    out_shape = [jax.ShapeDtypeStruct((n, D_MODEL), F32), jax.ShapeDtypeStruct((n, nkv), F32),
                 jax.ShapeDtypeStruct((n, nkv), BF), jax.ShapeDtypeStruct((nkv, n), BF),
                 jax.ShapeDtypeStruct((D_ATTN, n), BF), jax.ShapeDtypeStruct((ngn, n), F32)]
    out_specs = [pl.BlockSpec((tm, D_MODEL), lambda i: (i, 0)), pl.BlockSpec((tm, nkv), lambda i: (i, 0)),
                 pl.BlockSpec((tm, nkv), lambda i: (i, 0)), pl.BlockSpec((nkv, tm), lambda i: (0, i)),
                 pl.BlockSpec((D_ATTN, tm), lambda i: (0, i)), pl.BlockSpec((ngn, tm), lambda i: (0, i))]
    return pl.pallas_call(
        _attn_in_prompt_kernel, grid=(n // tm,), in_specs=in_specs, out_specs=out_specs, out_shape=out_shape,
        compiler_params=_cparams(("arbitrary",)), name="attn_in_prompt",
    )(x, wgb, wkv, wqt, wkvt, wgnt)


def _attn_in_sample_kernel(x_ref, wgb_ref, wkv_ref, wq_ref, wgn_ref, sgb_ref, kv_ref, q_ref, gn_ref):
    xb = x_ref[...].astype(BF)
    sgb_ref[...] = jax.nn.sigmoid(_dot(xb, wgb_ref[...]))
    kv_ref[...] = _dot(xb, wkv_ref[...])
    q_ref[...] = _dot(xb, wq_ref[...]) * (HEAD_DIM ** -0.5)
    gn_ref[...] = jax.nn.sigmoid(_dot(xb, wgn_ref[...]))


def _attn_in_sample_call(x, wgb, wkv, wq, wgn):
    n = x.shape[0]
    nkv = wkv.shape[1]
    out_shape = [jax.ShapeDtypeStruct((n, D_MODEL), F32), jax.ShapeDtypeStruct((n, nkv), F32),
                 jax.ShapeDtypeStruct((n, D_ATTN), F32), jax.ShapeDtypeStruct((n, wgn.shape[1]), F32)]
    return pl.pallas_call(
        _attn_in_sample_kernel, grid=(1,),
        in_specs=[_full(x.shape), _full(wgb.shape), _full(wkv.shape), _full(wq.shape), _full(wgn.shape)],
        out_specs=[_full(s.shape) for s in out_shape], out_shape=out_shape,
        compiler_params=_cparams(("arbitrary",)), name="attn_in_sample",
    )(x, wgb, wkv, wq, wgn)


def _compress_rows(src_refs, n_sub, wp_ref, w2p_ref, w2pt_ref, ppos_ref):
    outs = []
    for slot in range(2):
        src_ref = src_refs[slot]
        x = jnp.concatenate(
            [src_ref[pl.ds(s, n_sub, stride=CMP_STRIDE), :].astype(BF) for s in range(CMP_STRIDE)],
            axis=1)
        hh = _dot(x, wp_ref[slot])
        pp = _dot(ppos_ref[slot], wp_ref[slot])
        pos = pp[0:1, 0:2 * CMP_HIDDEN] + pp[1:2, 2 * CMP_HIDDEN:]
        h1 = pltpu.roll(hh[:, 2 * CMP_HIDDEN:], n_sub - 1, 0)
        g = jax.nn.gelu(hh[:, 0:2 * CMP_HIDDEN] + h1 + pos).astype(BF)
        if slot == 0:
            outs.append(_dot(g, w2p_ref[0]))
        else:
            outs.append(_dot_nt(w2pt_ref[1], g))
    return outs


def _compress_kernel(kcm_ref, vcm_ref, wp_ref, w2p_ref, w2pt_ref, ppos_ref, kc_ref, vct_ref):
    n_sub = kc_ref.shape[0]
    kc, vct = _compress_rows((kcm_ref, vcm_ref), n_sub, wp_ref, w2p_ref, w2pt_ref, ppos_ref)
    kc_ref[...] = kc.astype(BF)
    vct_ref[...] = vct.astype(BF)


def _compress_call(kv, cw, *, batch, seq):
    n_sub = seq // CMP_STRIDE
    wp, w2p, w2pt, ppos = cw
    return pl.pallas_call(
        _compress_kernel, grid=(batch,),
        in_specs=[pl.BlockSpec((seq, LANES), lambda b: (b, 0)), pl.BlockSpec((seq, LANES), lambda b: (b, 1)),
                  _full(wp.shape), _full(w2p.shape), _full(w2pt.shape), _full(ppos.shape)],
        out_specs=[pl.BlockSpec((None, n_sub, LANES), lambda b: (b, 0, 0)),
                   pl.BlockSpec((None, LANES, n_sub), lambda b: (b, 0, 0))],
        out_shape=[jax.ShapeDtypeStruct((batch, n_sub, LANES), BF), jax.ShapeDtypeStruct((batch, LANES, n_sub), BF)],
        compiler_params=_cparams(("arbitrary",)), name="compress_prompt",
    )(kv, kv, wp, w2p, w2pt, ppos)


def _top_blocks(score):
    n_blk = score.shape[0]
    jf = lax.broadcasted_iota(jnp.int32, score.shape, 0).astype(F32)
    work = score
    sel = jnp.zeros_like(score)
    for _ in range(N_SELECT):
        mx = jnp.max(work, axis=0, keepdims=True)
        first = jnp.min(jnp.where(work == mx, jf, float(n_blk)), axis=0, keepdims=True)
        pick = jf == first
        sel = jnp.where(pick, 1.0, sel)
        work = jnp.where(pick, -3e38, work)
    return sel


def _nsa_prompt_kernel(qt_ref, gn_ref, ksel_ref, kwin_ref, vselt_ref, vwint_ref, kc_ref, vct_ref, ovt_ref, eblk_ref,
                       o_ref, sel_ref, bias_ref, m_ref, acc_ref, qtp_ref, tot_ref, sa_ref, sb_ref):
    i = pl.program_id(1)
    q0 = i * Q_BLOCK
    seq = ksel_ref.shape[0]
    n_cmp = kc_ref.shape[0]
    n_selb = ovt_ref.shape[0]
    nq = Q_PER_KV * Q_BLOCK
    qpos = q0 + lax.broadcasted_iota(jnp.int32, (1, Q_BLOCK), 1)
    blk =lax.broadcasted_iota(jnp.int32, (n_selb, Q_BLOCK), 0)
    cur = qpos // SEL_BLOCK
    forced = (blk == 0) | (blk == cur) | (blk == cur - 1)
    valid = blk * SEL_BLOCK <= qpos
    krow = lax.broadcasted_iota(jnp.int32, (KEY_TILE, Q_BLOCK), 0)
    tcol = lax.broadcasted_iota(jnp.int32, (KEY_TILE, Q_BLOCK), 1)
    zeros_q = jnp.zeros((HEAD_DIM, nq), BF)
    ones = jnp.ones((ONES_ROWS, KEY_TILE), BF)

    chunks = [(h, c) for h in range(KV_HEADS) for c in range(nq // COL_CHUNK)]

    def chunk_cols(c):
        return slice(c * COL_CHUNK, (c + 1) * COL_CHUNK)

    blocks_per_tile = KEY_TILE // SEL_BLOCK
    phantom_bias = n_selb // blocks_per_tile

    def set_bias(bias_idx):
        rows = [bias_ref[pl.ds(bias_idx * blocks_per_tile + jb, 1), :] for jb in range(blocks_per_tile)]
        tile = jnp.concatenate(rows + [jnp.zeros((BIAS_ROWS - blocks_per_tile, KV_HEADS * Q_BLOCK), F32)], axis=0)
        for h in range(KV_HEADS):
            qtp_ref[h, LANES:LANES + BIAS_ROWS, :] = jnp.concatenate(
                [tile[:, h * Q_BLOCK:(h + 1) * Q_BLOCK]] * Q_PER_KV, axis=1).astype(BF)

    def score_chunk(dst_ref, k_tile, h, c):
        k_aug = jnp.concatenate([k_tile, eblk_ref[...]], axis=1)
        dst_ref[h, :, chunk_cols(c)] = _dot(k_aug, qtp_ref[h, :, chunk_cols(c)])

    def flash_tile(src_ref, dst_ref, k_next, vt_tiles, masks, next_bias=None):
        vt_augs = [jnp.concatenate([vt, ones], axis=0) for vt in vt_tiles]
        if next_bias is not None:
            set_bias(next_bias)
        for h, c in chunks:
            cols = chunk_cols(c)
            if k_next is not None:
                score_chunk(dst_ref, k_next, h, c)
            m_old = m_ref[h, :, cols]
            ps, m_news = [], []
            for g in range(COL_CHUNK // Q_BLOCK):
                lo = c * COL_CHUNK + g * Q_BLOCK
                s = src_ref[h, :, lo:lo + Q_BLOCK]
                if masks is not None:
                    s = jnp.where(masks[h], s, NEG)
                m_new = jnp.maximum(m_old[:, g * Q_BLOCK:(g + 1) * Q_BLOCK], jnp.max(s, axis=0, keepdims=True))
                ps.append(jnp.exp2(s - m_new).astype(BF))
                m_news.append(m_new)
            m_new = jnp.concatenate(m_news, axis=1)
            acc_ref[h, :, cols] = (jnp.exp2(m_old - m_new) * acc_ref[h, :, cols]
                                   + _dot(vt_augs[h], jnp.concatenate(ps, axis=1)))
            m_ref[h, :, cols] = m_new

    def flash_branch(k_ref, vt_ref, n_tiles, tile_start, tile_masks, tile_bias=None):
        def k_tile(t):
            k0 = pl.multiple_of(jnp.clip(tile_start(t), 0, seq - KEY_TILE), Q_BLOCK)
            return k_ref[pl.ds(k0, KEY_TILE), :]

        def vt_tiles(t):
            k0 = pl.multiple_of(jnp.clip(tile_start(t), 0, seq - KEY_TILE), Q_BLOCK)
            return [head_rows(vt_ref, h)[:, pl.ds(k0, KEY_TILE)] for h in range(KV_HEADS)]

        m_ref[...] = jnp.full(m_ref.shape, NEG, F32)
        acc_ref[...] = jnp.zeros(acc_ref.shape, F32)
        n_loop = n_tiles if tile_bias is None else n_tiles - 1

        def first_scores(t, bias_idx):
            if bias_idx is not None:
                set_bias(bias_idx)
            first = k_tile(t)
            for h, c in chunks:
                score_chunk(sa_ref, first, h, c)

        def loop_masks(t):
            return tile_masks(t) if tile_bias is None else None

        def loop_bias(t):
            return None if tile_bias is None else jnp.where(t < n_loop, tile_bias(t), phantom_bias)

        first_scores(0, loop_bias(0))

        def body(j, carry):
            flash_tile(sa_ref, sb_ref, k_tile(2 * j + 1), vt_tiles(2 * j), loop_masks(2 * j), loop_bias(2 * j + 1))
            flash_tile(sb_ref, sa_ref, k_tile(2 * j + 2), vt_tiles(2 * j + 1), loop_masks(2 * j + 1), loop_bias(2 * j + 2))
            return carry

        lax.fori_loop(0, (n_loop + 1) // 2, body, 0)
        if tile_bias is not None:
            last = n_tiles - 1
            first_scores(last, tile_bias(last))
            flash_tile(sa_ref, None, None, vt_tiles(last), tile_masks(last))

    def head_rows(ref, h):
        return ref.at[h * HEAD_DIM:(h + 1) * HEAD_DIM]

    def flash_result(h):
        return acc_ref[h, 0:HEAD_DIM, :] * (1.0 / acc_ref[h, HEAD_DIM:HEAD_DIM + 1, :])

    def gate(h, c):
        r = (h * 3 + c) * Q_PER_KV
        return jnp.concatenate([gn_ref[r + g:r + g + 1, :] for g in range(Q_PER_KV)], axis=1)

    for h in range(KV_HEADS):
        qh = qt_ref[h * Q_PER_KV * HEAD_DIM:(h + 1) * Q_PER_KV * HEAD_DIM, :]
        qcat = jnp.concatenate([qh[g * HEAD_DIM:(g + 1) * HEAD_DIM, :] for g in range(Q_PER_KV)], axis=1)
        qtp = jnp.concatenate([qcat, zeros_q] if h == 0 else [zeros_q, qcat], axis=0)
        qtp_ref[h, 0:LANES, :] = qtp
        qtp_ref[h, LANES:2 * LANES, :] = jnp.zeros((LANES, nq), BF)

    def compressed_branch(n_rows):
        cmp_end = lax.broadcasted_iota(jnp.int32, (n_rows, Q_BLOCK), 0) * CMP_STRIDE + (CMP_BLOCK - 1)
        vis = cmp_end <= qpos
        any_vis = qpos >= CMP_BLOCK - 1
        for h in range(KV_HEADS):
            sc = _dot(kc_ref[0:n_rows, :], qtp_ref[h, 0:LANES, :])
            psum = jnp.zeros((n_rows, Q_BLOCK), F32)
            pcols = []
            for g in range(Q_PER_KV):
                s = jnp.where(vis, sc[:, g * Q_BLOCK:(g + 1) * Q_BLOCK], NEG)
                e = jnp.exp2(s - jnp.max(s, axis=0, keepdims=True))
                inv = jnp.where(any_vis, 1.0 / jnp.sum(e, axis=0, keepdims=True), 0.0)
                p = e * inv
                psum = psum + p
                pcols.append(p.astype(BF))
            tot_ref[h] = gate(h, 0) * _dot(vct_ref[h * HEAD_DIM:(h + 1) * HEAD_DIM, 0:n_rows],
                                           jnp.concatenate(pcols, axis=1))
            p_hi = psum.astype(BF)
            p_lo = (psum - p_hi.astype(F32)).astype(BF)
            imp = _dot(ovt_ref[:, 0:n_rows], p_hi) + _dot(ovt_ref[:, 0:n_rows], p_lo)
            sel_ref[:, h * Q_BLOCK:(h + 1) * Q_BLOCK] = jnp.where(valid, imp + jnp.where(forced, FORCE_BONUS, 0.0), NEG)

    vis_step = min(n_cmp, CMP_VIS_ROWS)
    last_vis = (q0 + Q_BLOCK - CMP_BLOCK) // CMP_STRIDE
    for c in range(n_cmp // vis_step):
        pl.when(last_vis // vis_step == c)(functools.partial(compressed_branch, (c + 1) * vis_step))

    picked = (_top_blocks(sel_ref[...]) > 0.5) & jnp.concatenate([valid] * KV_HEADS, axis=1)
    bias_ref[0:n_selb, :] = jnp.where(picked, 0.0, NEG)
    bias_ref[n_selb:n_selb + SUBLANES, :] = jnp.full((SUBLANES, KV_HEADS * Q_BLOCK), NEG, F32)

    def sel_masks(t):
        return [t * KEY_TILE + krow <= q0 + tcol] * KV_HEADS

    flash_branch(ksel_ref, vselt_ref, (q0 + Q_BLOCK + KEY_TILE - 1) // KEY_TILE, lambda t: t * KEY_TILE, sel_masks,
                 tile_bias=lambda t: t)
    for h in range(KV_HEADS):
        tot_ref[h] = tot_ref[h] + gate(h, 1) * flash_result(h)
        qtp_ref[h, LANES:LANES + BIAS_ROWS, :] = jnp.zeros((BIAS_ROWS, nq), BF)

    n_wt = (WINDOW + Q_BLOCK + KEY_TILE - 1) // KEY_TILE
    w0 = jnp.clip(q0 - WINDOW, 0, seq - n_wt * KEY_TILE)

    def win_masks(t):
        rel = (q0 + tcol) - (w0 + t * KEY_TILE + krow)
        return [(rel >= 0) & (rel < WINDOW)] * KV_HEADS

    flash_branch(kwin_ref, vwint_ref, n_wt, lambda t: w0 + t * KEY_TILE, win_masks)

    for h in range(KV_HEADS):
        tot = tot_ref[h] + gate(h, 2) * flash_result(h)
        for g in range(Q_PER_KV):
            r = (h * Q_PER_KV + g) * HEAD_DIM
            o_ref[r:r + HEAD_DIM, :] = tot[:, g * Q_BLOCK:(g + 1) * Q_BLOCK].astype(BF)


def _nsa_prompt_call(qt, gnt, kvb, kvt, kc, vct, ovt, *, batch, seq):
    nblk = seq // Q_BLOCK
    eblk = jnp.asarray(np.arange(LANES)[None, :] == (np.arange(KEY_TILE) // SEL_BLOCK)[:, None], BF)
    n = batch * seq
    n_sub = seq // CMP_STRIDE
    n_selb = seq // SEL_BLOCK
    nq = Q_PER_KV * Q_BLOCK
    in_specs = [
        pl.BlockSpec((D_ATTN, Q_BLOCK), lambda b, i: (0, b * nblk + i)),
        pl.BlockSpec((gnt.shape[0], Q_BLOCK), lambda b, i: (0, b * nblk + i)),
        pl.BlockSpec((seq, LANES), lambda b, i: (b, 2)),
        pl.BlockSpec((seq, LANES), lambda b, i: (b, 4)),
        pl.BlockSpec((LANES, seq), lambda b, i: (3, b)),
        pl.BlockSpec((LANES, seq), lambda b, i: (5, b)),
        pl.BlockSpec((None, n_sub, LANES), lambda b, i: (b, 0, 0)),
        pl.BlockSpec((None, LANES, n_sub), lambda b, i: (b, 0, 0)),
        _full(ovt.shape),
        _full(eblk.shape),
    ]
    return pl.pallas_call(
        _nsa_prompt_kernel, grid=(batch, nblk), in_specs=in_specs,
        out_specs=pl.BlockSpec((D_ATTN, Q_BLOCK), lambda b, i: (0, b * nblk + i)),
        out_shape=jax.ShapeDtypeStruct((D_ATTN, n), BF),
        scratch_shapes=[pltpu.VMEM((n_selb, KV_HEADS * Q_BLOCK), F32),
                        pltpu.VMEM((n_selb + SUBLANES, KV_HEADS * Q_BLOCK), F32), pltpu.VMEM((KV_HEADS, 1, nq), F32),
                        pltpu.VMEM((KV_HEADS, HEAD_DIM + ONES_ROWS, nq), F32),
                        pltpu.VMEM((KV_HEADS, 2 * LANES, nq), BF),
                        pltpu.VMEM((KV_HEADS, HEAD_DIM, nq), F32),
                        pltpu.VMEM((KV_HEADS, KEY_TILE, nq), F32), pltpu.VMEM((KV_HEADS, KEY_TILE, nq), F32)],
        compiler_params=_cparams(("arbitrary", "arbitrary")), name="nsa_prompt",
    )(qt, gnt, kvb, kvb, kvt, kvt, kc, vct, ovt, eblk)


def _route(x1, wrt_ref, br_ref, tri_ref, cnt_ref):
    tm = x1.shape[0]
    logits = lax.dot_general(wrt_ref[...], x1, (((1,), (1,)), ((), ())), precision=lax.Precision.HIGHEST,
                             preferred_element_type=F32)
    aff = jax.nn.sigmoid(logits)
    grp = aff + br_ref[...]
    affr = [aff[k:k + 1, :] for k in range(N_EXPERTS)]
    grpr = [grp[k:k + 1, :] for k in range(N_EXPERTS)]
    best = None
    gsel = jnp.zeros((1, tm), jnp.int32)
    for gi in range(N_EXPERT_GROUPS):
        m = grpr[gi * 4:(gi + 1) * 4]
        top2 = None
        for a in range(4):
            for b in range(a + 1, 4):
                s = m[a] + m[b]
                top2 = s if top2 is None else jnp.maximum(top2, s)
        if best is None:
            best = top2
        else:
            better = top2 > best
            gsel = jnp.where(better, gi, gsel)
            best = jnp.where(better, top2, best)
    ing, ina = [], []
    for j in range(4):
        vg, va = grpr[j], affr[j]
        for gi in range(1, N_EXPERT_GROUPS):
            vg = jnp.where(gsel == gi, grpr[gi * 4 + j], vg)
            va = jnp.where(gsel == gi, affr[gi * 4 + j], va)
        ing.append(vg)
        ina.append(va)

    def argmax4(vals):
        bv, bi = vals[0], jnp.zeros((1, tm), jnp.int32)
        for j in range(1, 4):
            better = vals[j] > bv
            bi = jnp.where(better, j, bi)
            bv = jnp.where(better, vals[j], bv)
        return bi

    loc1 = argmax4(ing)
    loc2 = argmax4([jnp.where(loc1 == j, -3e38, ing[j]) for j in range(4)])
    w1 = sum(jnp.where(loc1 == j, ina[j], 0.0) for j in range(4))
    w2 = sum(jnp.where(loc2 == j, ina[j], 0.0) for j in range(4))
    tot = w1 + w2
    wd = [jnp.where(loc1 == j, w1 / tot, 0.0) + jnp.where(loc2 == j, w2 / tot, 0.0) for j in range(4)]
    oh = jnp.concatenate([(gsel == gi).astype(F32) for gi in range(N_EXPERT_GROUPS)]
                         + [jnp.zeros((SUBLANES - N_EXPERT_GROUPS, tm), F32)], axis=0)
    cum = _dot(oh.astype(BF), tri_ref[...])
    carry = cnt_ref[...][:, 0:1]
    rank = jnp.sum(oh * (cum + carry), axis=0, keepdims=True)
    cnt_ref[...] = cnt_ref[...] + jnp.sum(oh, axis=1, keepdims=True)
    rows = wd + [gsel.astype(F32), rank, jnp.zeros((ROUTE_W - 6, tm), F32)]
    return jnp.concatenate(rows, axis=0)


def _merge_kernel(x_ref, ya_ref, sgb_ref, o_ref, wpb_ref, wo_ref, g_ref, b_ref, wrt_ref, br_ref, tri_ref,
                  xr_ref, cnt_out_ref, cnt_ref, *, alpha, o_transposed):
    @pl.when(pl.program_id(0) == 0)
    def _():
        cnt_ref[...] = jnp.zeros(cnt_ref.shape, F32)

    if o_transposed:
        ob = _dot_tn(o_ref[...], wpb_ref[...])
    else:
        ob = _dot(o_ref[...].astype(BF), wpb_ref[...])
    y = ya_ref[...] + sgb_ref[...] * ob
    mix = _dot(y.astype(BF), wo_ref[...])
    x1 = _layer_norm(alpha * x_ref[...] + mix, g_ref[...], b_ref[...])
    xr_ref[:, 0:D_MODEL] = x1
    info = _route(x1, wrt_ref, br_ref, tri_ref, cnt_ref)
    xr_ref[:, D_MODEL:XR_W] = info.T
    cnt_out_ref[...] = cnt_ref[...]


def _merge_call(x, ya, sgb, o, wpb, wo, g, b, wrt, br, tri, *, tm, alpha, o_transposed):
    n = ya.shape[0]
    tok = pl.BlockSpec((tm, D_MODEL), lambda i: (i, 0))
    ospec = pl.BlockSpec((D_ATTN, tm), lambda i: (0, i)) if o_transposed else tok
    return pl.pallas_call(
        functools.partial(_merge_kernel, alpha=alpha, o_transposed=o_transposed), grid=(n // tm,),
        in_specs=[tok, tok, tok, ospec, _full(wpb.shape), _full(wo.shape), _full(g.shape), _full(b.shape),
                  _full(wrt.shape), _full(br.shape), _full(tri.shape)],
        out_specs=[pl.BlockSpec((tm, XR_W), lambda i: (i, 0)), _full((SUBLANES, LANES))],
        out_shape=[jax.ShapeDtypeStruct((n, XR_W), F32), jax.ShapeDtypeStruct((SUBLANES, LANES), F32)],
        scratch_shapes=[pltpu.VMEM((SUBLANES, LANES), F32)],
        compiler_params=_cparams(("arbitrary",)), name="merge_t" if o_transposed else "merge_n",
    )(x, ya, sgb, o, wpb, wo, g, b, wrt, br, tri)


def _row_copy(src_ref, dst_ref, sem, src_row, dst_row):
    return pltpu.make_async_copy(src_ref.at[pl.ds(src_row, 1), :], dst_ref.at[pl.ds(dst_row, 1), :], sem)


def _scatter_kernel(dest_ref, xr_ref, xs_in_ref, xs_ref, sem):
    del xs_in_ref
    tm = xr_ref.shape[0]

    def start(grp, c):
        for u in range(ROW_DMA_UNROLL):
            r = grp * ROW_DMA_UNROLL + u
            _row_copy(xr_ref, xs_ref, sem, r, dest_ref[r]).start(priority=u % 2)
        return c

    def wait(r, c):
        _row_copy(xr_ref, xs_ref, sem, r, dest_ref[r]).wait()
        return c

    lax.fori_loop(0, tm // ROW_DMA_UNROLL, start, 0)
    lax.fori_loop(0, tm, wait, 0, unroll=ROW_DMA_UNROLL)


def _scatter_call(dest, xr, xs0, *, tm):
    n = xr.shape[0]
    return pl.pallas_call(
        _scatter_kernel, grid=(n // tm,),
        in_specs=[pl.BlockSpec((tm,), lambda i: (i,), memory_space=pltpu.SMEM),
                  pl.BlockSpec((tm, XR_W), lambda i: (i, 0)), pl.BlockSpec(memory_space=pl.ANY)],
        out_specs=pl.BlockSpec(memory_space=pl.ANY),
        out_shape=jax.ShapeDtypeStruct(xs0.shape, F32),
        scratch_shapes=[pltpu.SemaphoreType.DMA(())],
        input_output_aliases={2: 0},
        compiler_params=_cparams(("arbitrary",)), name="moe_scatter",
    )(dest, xr, xs0)


def _experts_kernel(tg_ref, nu_ref, xs_ref, wg_ref, wu_ref, wd_ref, ys_ref):
    del tg_ref
    j = pl.program_id(0)

    @pl.when(j < nu_ref[0])
    def _():
        xb = xs_ref[:, 0:D_MODEL].astype(BF)
        acc = None
        for e in range(EXPERTS_PER_GROUP):
            hidden = jax.nn.silu(_dot(xb, wg_ref[e])) * _dot(xb, wu_ref[e])
            hidden = hidden * xs_ref[:, D_MODEL + e:D_MODEL + e + 1]
            y = _dot(hidden.astype(BF), wd_ref[e])
            acc = y if acc is None else acc + y
        ys_ref[...] = acc

    @pl.when(j >= nu_ref[0])
    def _():
        ys_ref[...] = jnp.zeros(ys_ref.shape, F32)


def _experts_call(tile_group, n_used, xs, wg, wu, wd, *, tmx):
    rows = xs.shape[0]
    wspec_in = pl.BlockSpec((None, EXPERTS_PER_GROUP, D_MODEL, D_EXPERT), lambda j, tg, nu: (tg[j], 0, 0, 0))
    wspec_out = pl.BlockSpec((None, EXPERTS_PER_GROUP, D_EXPERT, D_MODEL), lambda j, tg, nu: (tg[j], 0, 0, 0))
    grid_spec = pltpu.PrefetchScalarGridSpec(
        num_scalar_prefetch=2, grid=(rows // tmx,),
        in_specs=[pl.BlockSpec((tmx, XR_W), lambda j, tg, nu: (j, 0)), wspec_in, wspec_in, wspec_out],
        out_specs=pl.BlockSpec((tmx, D_MODEL), lambda j, tg, nu: (j, 0)))
    return pl.pallas_call(
        _experts_kernel, grid_spec=grid_spec, out_shape=jax.ShapeDtypeStruct((rows, D_MODEL), F32),
        compiler_params=_cparams(("arbitrary",)), name="moe_experts",
    )(tile_group, n_used, xs, wg, wu, wd)


def _combine_kernel(dest_ref, xr_ref, ys_ref, g_ref, b_ref, x2_ref, buf_ref, sem, *, alpha):
    tm = xr_ref.shape[0]

    def start(grp, c):
        for u in range(ROW_DMA_UNROLL):
            r = grp * ROW_DMA_UNROLL + u
            _row_copy(ys_ref, buf_ref, sem, dest_ref[r], r).start(priority=u % 2)
        return c

    def wait(r, c):
        _row_copy(ys_ref, buf_ref, sem, dest_ref[r], r).wait()
        return c

    lax.fori_loop(0, tm // ROW_DMA_UNROLL, start, 0)
    lax.fori_loop(0, tm, wait, 0, unroll=ROW_DMA_UNROLL)
    x2_ref[...] = _layer_norm(alpha * xr_ref[...] + buf_ref[...], g_ref[...], b_ref[...])


def _combine_call(dest, xr, ys, g, b, *, tm, alpha):
    n = xr.shape[0]
    return pl.pallas_call(
        functools.partial(_combine_kernel, alpha=alpha), grid=(n // tm,),
        in_specs=[pl.BlockSpec((tm,), lambda i: (i,), memory_space=pltpu.SMEM),
                  pl.BlockSpec((tm, D_MODEL), lambda i: (i, 0)), pl.BlockSpec(memory_space=pl.ANY),
                  _full(g.shape), _full(b.shape)],
        out_specs=pl.BlockSpec((tm, D_MODEL), lambda i: (i, 0)),
        out_shape=jax.ShapeDtypeStruct((n, D_MODEL), F32),
        scratch_shapes=[pltpu.VMEM((tm, D_MODEL), F32), pltpu.SemaphoreType.DMA(())],
        compiler_params=_cparams(("arbitrary",)), name="moe_combine",
    )(dest, xr, ys, g, b)


def _moe(xr, cnt, lw, *, tm, tmx, alpha):
    n = xr.shape[0]
    n_tiles = n // tmx + N_EXPERT_GROUPS
    counts = cnt[:N_EXPERT_GROUPS, 0].astype(jnp.int32)
    tiles_per = (counts + tmx - 1) // tmx
    tile_end = jnp.cumsum(tiles_per)
    offs = (tile_end - tiles_per) * tmx
    gid = xr[:, D_MODEL + 4].astype(jnp.int32)
    rank = xr[:, D_MODEL + 5].astype(jnp.int32)
    dest = offs[gid] + rank
    tile_group = jnp.minimum(jnp.sum(jnp.arange(n_tiles)[:, None] >= tile_end[None, :], axis=1),
                             N_EXPERT_GROUPS - 1).astype(jnp.int32)
    n_used = tile_end[-1:].astype(jnp.int32)
    xs = _scatter_call(dest, xr, jnp.zeros((n_tiles * tmx, XR_W), F32), tm=tm)
    ys = _experts_call(tile_group, n_used, xs, lw["wg"], lw["wu_e"], lw["wd"], tmx=tmx)
    return _combine_call(dest, xr, ys, lw["ln2_g"], lw["ln2_b"], tm=tm, alpha=alpha)


def _softmax_with_new(s, mask, s_new, new_on):
    s = jnp.where(mask, s, NEG)
    s_new = jnp.where(new_on, s_new, NEG)
    mx = jnp.maximum(jnp.max(s, axis=1, keepdims=True), s_new)
    p = jnp.where(mask, jnp.exp(s - mx), 0.0)
    p_new = jnp.where(new_on, jnp.exp(s_new - mx), 0.0)
    den = jnp.maximum(jnp.sum(p, axis=1, keepdims=True) + p_new, 1e-30)
    return p, p_new, 1.0 / den


def _sample_attention(qs, news, gates, cw_refs, x_refs, sel_refs, wp_ref, w2p_ref, w2pt_ref, ppos_ref, ov_ref,
                      exp_ref, *, n_past):
    n_seq = len(qs)
    seqs = range(n_seq)
    n_sub = n_past // CMP_STRIDE
    n_cmp = n_sub - CMP_BLOCK // CMP_STRIDE + 1
    nrow = KV_HEADS * Q_PER_KV
    qpos = n_past
    qfs = [q.astype(F32) for q in qs]

    def new_score(k, col):
        kn = news[k][:, col * LANES:(col + 1) * LANES].astype(BF).astype(F32)
        return jnp.sum(qfs[k] * kn, axis=1, keepdims=True)

    def new_value(k, col):
        return news[k][:, col * LANES:(col + 1) * LANES].astype(BF).astype(F32)

    def rows(x, idx, n):
        return x[idx * n:(idx + 1) * n]

    g_all = []
    for slot in range(2):
        w = wp_ref[slot]
        pp = _dot(ppos_ref[slot], w)
        pos = pp[0:1, 0:2 * CMP_HIDDEN] + pp[1:2, 2 * CMP_HIDDEN:]
        x_all = jnp.concatenate(
            [jnp.concatenate([x_refs[k][slot, pl.ds(s, n_sub, stride=CMP_STRIDE), :].astype(BF)
                              for s in range(CMP_STRIDE)], axis=1) for k in seqs], axis=0)
        hh_all = _dot(x_all, w)
        g_seq = []
        for k in seqs:
            hh = rows(hh_all, k, n_sub)
            h1 = pltpu.roll(hh[:, 2 * CMP_HIDDEN:], n_sub - 1, 0)
            g_seq.append(jax.nn.gelu(hh[:, 0:2 * CMP_HIDDEN] + h1 + pos).astype(BF))
        g_all.append(jnp.concatenate(g_seq, axis=0))
    kc_all = _dot(g_all[0], w2p_ref[0]).astype(BF)
    vct_all = _dot_nt(w2pt_ref[1], g_all[1]).astype(BF)

    n_idx = lax.broadcasted_iota(jnp.int32, (nrow, n_sub), 1)
    vis = (n_idx * CMP_STRIDE + (CMP_BLOCK - 1) <= qpos) & (n_idx < n_cmp)
    s_cs = [_dot_nt(qs[k], rows(kc_all, k, n_sub)) for k in seqs]
    p_cs = []
    for k in seqs:
        s_c = jnp.where(vis, s_cs[k], NEG)
        mx = jnp.max(s_c, axis=1, keepdims=True)
        p_c = jnp.where(vis, jnp.exp(s_c - mx), 0.0)
        p_cs.append(p_c * (1.0 / jnp.maximum(jnp.sum(p_c, axis=1, keepdims=True), 1e-30)))
    o_cs = [_dot_nt(p_cs[k].astype(BF), vct_all[:, k * n_sub:(k + 1) * n_sub]) for k in seqs]

    psum = jnp.concatenate([jnp.sum(p_cs[k][h * Q_PER_KV:(h + 1) * Q_PER_KV], axis=0, keepdims=True)
                            for k in seqs for h in range(KV_HEADS)], axis=0)
    imp_rows = -(-psum.shape[0] // SUBLANES) * SUBLANES
    if imp_rows > psum.shape[0]:
        psum = jnp.concatenate([psum, jnp.zeros((imp_rows - psum.shape[0], n_sub), F32)], axis=0)
    p_hi = psum.astype(BF)
    p_lo = (psum - p_hi.astype(F32)).astype(BF)
    imp = _dot(p_hi, ov_ref[...]) + _dot(p_lo, ov_ref[...])
    n_selp = imp.shape[1]
    j_idx = lax.broadcasted_iota(jnp.int32, (imp_rows, n_selp), 1)
    cur = qpos // SEL_BLOCK
    forced = (j_idx == 0) | (j_idx == cur) | (j_idx == cur - 1)
    valid = j_idx * SEL_BLOCK <= qpos
    score = jnp.where(valid, imp + jnp.where(forced, FORCE_BONUS, 0.0), NEG)
    score_t = jnp.concatenate([score, jnp.zeros((n_selp - imp_rows, n_selp), F32)], axis=0).T
    jr = lax.broadcasted_iota(jnp.int32, (n_selp, n_selp), 0)
    jc = lax.broadcasted_iota(jnp.int32, (n_selp, n_selp), 1)
    sel_rows = []
    for r in range(n_seq * KV_HEADS):
        other = jnp.broadcast_to(score_t[:, r:r + 1], (n_selp, n_selp))
        mine = jnp.broadcast_to(score[r:r + 1, :], (n_selp, n_selp))
        ahead = (other > mine) | ((other == mine) & (jr < jc))
        rank = jnp.sum(ahead.astype(F32), axis=0, keepdims=True)
        sel_rows.append(jnp.broadcast_to((rank < N_SELECT).astype(F32), (Q_PER_KV, n_selp)))
    sel = jnp.concatenate(sel_rows, axis=0)

    on_all = _dot(sel.astype(BF), exp_ref[...]) > 0.5
    s_ss = [_dot_nt(qs[k], sel_refs[k][:, 0:LANES].astype(BF)) for k in seqs]
    soft = [_softmax_with_new(s_ss[k], rows(on_all, k, nrow), new_score(k, 2), rows(sel, k, nrow)[:, cur:cur + 1] > 0.5)
            for k in seqs]
    pv_s = [_dot(soft[k][0].astype(BF), sel_refs[k][:, LANES:2 * LANES].astype(BF)) for k in seqs]
    o_ss = [(pv_s[k] + soft[k][1] * new_value(k, 3)) * soft[k][2] for k in seqs]

    wb = cw_refs[0].shape[0]
    wpos = (n_past - wb) + lax.broadcasted_iota(jnp.int32, (nrow, wb), 1)
    rel = qpos - wpos
    wmask = (rel >= 0) & (rel < WINDOW) & (wpos >= 0)
    s_ws = [_dot_nt(qs[k], cw_refs[k][:, 0:LANES].astype(BF)) for k in seqs]
    soft = [_softmax_with_new(s_ws[k], wmask, new_score(k, 4), jnp.full((nrow, 1), True)) for k in seqs]
    pv_w = [_dot(soft[k][0].astype(BF), cw_refs[k][:, LANES:2 * LANES].astype(BF)) for k in seqs]
    o_ws = [(pv_w[k] + soft[k][1] * new_value(k, 5)) * soft[k][2] for k in seqs]

    return [gates[k][:, 0:1] * o_cs[k] + gates[k][:, 1:2] * o_ss[k] + gates[k][:, 2:3] * o_ws[k] for k in seqs]


SEQ_PER_STEP = 4


def _nsa_sample_kernel(pt_ref, q_ref, new_ref, gate_ref, cw_ref, cache_ref, wp_ref, w2p_ref, w2pt_ref,
                       ppos_ref, ov_ref, exp_ref, o_ref, x_buf, sel_buf, sems, *, layer, n_pages, n_past):
    j = pl.program_id(0)
    n_steps = pl.num_programs(0)
    cur = lax.rem(j, 2)

    def step_copies(buf_set, step):
        copies = []
        for k in range(SEQ_PER_STEP):
            for p in range(n_pages):
                page = cache_ref.at[pt_ref[(step * SEQ_PER_STEP + k) * n_pages + p], layer]
                dst_rows = pl.ds(p * PAGE_SIZE, PAGE_SIZE)
                for slot in range(2):
                    copies.append(pltpu.make_async_copy(page.at[:, pl.ds(slot * LANES, LANES)],
                                                        x_buf.at[buf_set, k, slot, dst_rows, :], sems.at[buf_set]))
                copies.append(pltpu.make_async_copy(page.at[:, pl.ds(2 * LANES, 2 * LANES)],
                                                    sel_buf.at[buf_set, k, dst_rows, :], sems.at[buf_set]))
        return copies

    @pl.when(j == 0)
    def _():
        for c in step_copies(cur, j):
            c.start()

    @pl.when(j + 1 < n_steps)
    def _():
        for c in step_copies(1 - cur, j + 1):
            c.start()

    for c in step_copies(cur, j):
        c.wait()
    ks = range(SEQ_PER_STEP)
    outs = _sample_attention([q_ref[k] for k in ks], [new_ref[k] for k in ks], [gate_ref[k] for k in ks],
                             [cw_ref.at[k] for k in ks], [x_buf.at[cur, k] for k in ks],
                             [sel_buf.at[cur, k] for k in ks], wp_ref, w2p_ref, w2pt_ref, ppos_ref, ov_ref, exp_ref,
                             n_past=n_past)
    for k in ks:
        o_ref[k] = outs[k]


def _nsa_sample_call(pt, qp, new, gates, cwin, cache, cw, ov, expand, *, layer, n_past):
    n_dec = qp.shape[0]
    n_pages = n_past // PAGE_SIZE
    nrow = KV_HEADS * Q_PER_KV
    wb = cwin.shape[2]
    wp, w2p, w2pt, ppos = cw
    const = lambda shape: pl.BlockSpec(shape, lambda b, pt_: (0,) * len(shape))
    per_seq = lambda *tail: pl.BlockSpec((SEQ_PER_STEP,) + tail, lambda b, pt_: (b,) + (0,) * len(tail))
    grid_spec = pltpu.PrefetchScalarGridSpec(
        num_scalar_prefetch=1, grid=(n_dec // SEQ_PER_STEP,),
        in_specs=[per_seq(nrow, LANES), per_seq(1, new.shape[2]), per_seq(nrow, LANES),
                  pl.BlockSpec((SEQ_PER_STEP, None, wb, 2 * LANES), lambda b, pt_: (b, layer, 0, 0)),
                  pl.BlockSpec(memory_space=pl.ANY),
                  const(wp.shape), const(w2p.shape), const(w2pt.shape), const(ppos.shape), const(ov.shape),
                  const(expand.shape)],
        out_specs=per_seq(nrow, LANES),
        scratch_shapes=[pltpu.VMEM((2, SEQ_PER_STEP, 2, n_past, LANES), F32),
                        pltpu.VMEM((2, SEQ_PER_STEP, n_past, 2 * LANES), F32),
                        pltpu.SemaphoreType.DMA((2,))])
    return pl.pallas_call(
        functools.partial(_nsa_sample_kernel, layer=layer, n_pages=n_pages, n_past=n_past),
        grid_spec=grid_spec, out_shape=jax.ShapeDtypeStruct((n_dec, nrow, LANES), F32),
        compiler_params=_cparams(("arbitrary",)), name="nsa_sample",
    )(pt, qp, new, gates, cwin, cache, wp, w2p, w2pt, ppos, ov, expand)


def _overlap(n_cmp, n_sel):
    c0 = np.arange(n_cmp) * CMP_STRIDE
    s0 = np.arange(n_sel) * SEL_BLOCK
    return ((c0[:, None] < s0[None, :] + SEL_BLOCK) & (c0[:, None] + CMP_BLOCK > s0[None, :])).astype(np.float32)


def _layer_weights(l, w_in, gmlp_ln_g, gmlp_ln_b, gmlp_ws, gmlp_bs, cmp_pos, cmp_w1, cmp_w2, w_branch_a, w_branch_b,
                   w_out, ln1_g, ln1_b, ln2_g, ln2_b, w_gate, w_up, w_down):
    cuts = np.cumsum((0,) + PROJ_SIZES)
    seg = lambda i, j=None: w_in[l][:, cuts[i]:cuts[i + 1 if j is None else j]].astype(BF)
    wgn = seg(9)
    eye = jnp.eye(KV_HEADS, dtype=F32)
    r = CMP_BLOCK // CMP_STRIDE
    w1r = cmp_w1[l].reshape(2, r, CMP_STRIDE, HEAD_DIM, CMP_HIDDEN)
    wp = jnp.einsum("xrsdf,hk->xshdrkf", w1r, eye).reshape(2, CMP_STRIDE * D_KV, r * KV_HEADS * CMP_HIDDEN)
    w2p = jnp.einsum("xfd,hk->xhfkd", cmp_w2[l], eye).reshape(2, KV_HEADS * CMP_HIDDEN, D_KV)
    pos = cmp_pos[l].reshape(2, r, CMP_STRIDE, 1, HEAD_DIM)
    ppos = jnp.broadcast_to(pos, (2, r, CMP_STRIDE, KV_HEADS, HEAD_DIM)).reshape(2, r, CMP_STRIDE * D_KV)
    ppos = jnp.pad(ppos, ((0, 0), (0, SUBLANES - r), (0, 0)))
    ex = lambda w: w.reshape((N_EXPERT_GROUPS, EXPERTS_PER_GROUP) + w.shape[1:]).astype(BF)
    row = lambda v: v[l][None, :].astype(F32)
    return {
        "wu": seg(0), "wv": seg(1), "wq": seg(2), "wkv": seg(3, 9), "wga": seg(10), "wgb": seg(11),
        "wqt": seg(2).T, "wkvt": seg(3, 9).T,
        "wgnt": wgn.reshape(D_MODEL, KV_HEADS, Q_PER_KV, 3).transpose(1, 3, 2, 0).reshape(3 * N_HEADS, D_MODEL),
        "wgn": jnp.pad(wgn, ((0, 0), (0, LANES - 3 * N_HEADS))),
        "gln_g": row(gmlp_ln_g), "gln_b": row(gmlp_ln_b),
        "ws": gmlp_ws[l], "bs_t": gmlp_bs[l].T,
        "ws0": jnp.repeat(gmlp_ws[l][:, 0, 0], CHUNK)[None, :], "bs0": jnp.repeat(gmlp_bs[l][:, 0], CHUNK)[None, :],
        "cw": (wp.astype(BF), w2p.astype(BF), w2p.transpose(0, 2, 1).astype(BF), ppos.astype(BF)),
        "wpa": w_branch_a[l].astype(BF), "wpb": w_branch_b[l].astype(BF), "wo": w_out[l].astype(BF),
        "ln1_g": row(ln1_g), "ln1_b": row(ln1_b), "ln2_g": row(ln2_g), "ln2_b": row(ln2_b),
        "wg": ex(w_gate[l]), "wu_e": ex(w_up[l]), "wd": ex(w_down[l]),
    }


def _strict_upper(n):
    return jnp.asarray(np.triu(np.ones((n, n), np.float32), 1), BF)


def kernel(x_prompt, x_sample, cache_kv, cache_win, page_table, w_in, gmlp_ln_g, gmlp_ln_b, gmlp_ws, gmlp_bs,
           cmp_pos, cmp_w1, cmp_w2, w_branch_a, w_branch_b, w_out, ln1_g, ln1_b, ln2_g, ln2_b,
           w_router, b_router, w_gate, w_up, w_down):
    depth = w_in.shape[0]
    alpha = (2.0 * depth) ** 0.25
    batch, seq, _ = x_prompt.shape
    n_dec, t_dec = x_sample.shape[:2]
    n_past = page_table.shape[1] * PAGE_SIZE
    assert t_dec == 1 and seq % (4 * CHUNK) == 0 and n_past % PAGE_SIZE == 0 and n_dec % SUBLANES == 0
    tm_p, tmx_p = 4 * CHUNK, 2 * CHUNK
    tm_s = tmx_s = n_dec
    wb = cache_win.shape[2]

    n_sub_p = seq // CMP_STRIDE
    ovt_p = jnp.asarray(np.pad(_overlap(n_sub_p - 1, seq // SEL_BLOCK), ((0, 1), (0, 0))).T, BF)
    n_sub_s = n_past // CMP_STRIDE
    n_sel_s = n_past // SEL_BLOCK + 1
    ov_s = jnp.asarray(np.pad(_overlap(n_sub_s - 1, n_sel_s), ((0, 1), (0, LANES - n_sel_s))), BF)
    expand = jnp.asarray(np.arange(LANES)[:, None] == (np.arange(n_past) // SEL_BLOCK)[None, :], BF)
    wrt = w_router.T.astype(F32)
    br = b_router[:, None].astype(F32)
    tri_p, tri_s = _strict_upper(tm_p), _strict_upper(tm_s)
    cache = cache_kv.reshape(cache_kv.shape[0], depth, PAGE_SIZE, N_KV_SLOTS * D_KV)
    cwin = cache_win.reshape(n_dec, depth, wb, 2 * D_KV)
    pt = page_table.reshape(-1).astype(jnp.int32)

    xp = x_prompt.reshape(batch * seq, D_MODEL)
    xs = x_sample.reshape(n_dec, D_MODEL)
    kv_p, kv_s, win_s, gv_s = [], [], [], []
    for l in range(depth):
        lw = _layer_weights(l, w_in, gmlp_ln_g, gmlp_ln_b, gmlp_ws, gmlp_bs, cmp_pos, cmp_w1, cmp_w2, w_branch_a,
                            w_branch_b, w_out, ln1_g, ln1_b, ln2_g, ln2_b, w_gate, w_up, w_down)
        (ya,) = _gmlp_call(xp, lw["wu"], lw["wv"], lw["wga"], lw["wpa"], lw["gln_g"], lw["gln_b"], lw["ws"],
                           lw["bs_t"], tm=tm_p, sample=False)
        sgb, kv, kvb, kvt, qt, gnt = _attn_in_prompt_call(xp, lw["wgb"], lw["wkv"], lw["wqt"], lw["wkvt"], lw["wgnt"],
                                                          tm=tm_p)
        kc, vct = _compress_call(kv, lw["cw"], batch=batch, seq=seq)
        ot = _nsa_prompt_call(qt, gnt, kvb, kvt, kc, vct, ovt_p, batch=batch, seq=seq)
        xr, cnt = _merge_call(xp, ya, sgb, ot, lw["wpb"], lw["wo"], lw["ln1_g"], lw["ln1_b"], wrt, br, tri_p,
                              tm=tm_p, alpha=alpha, o_transposed=True)
        xp = _moe(xr, cnt, lw, tm=tm_p, tmx=tmx_p, alpha=alpha)
        kv_p.append(kv)
        ya, vn = _gmlp_call(xs, lw["wu"], lw["wv"], lw["wga"], lw["wpa"], lw["gln_g"], lw["gln_b"], lw["ws0"],
                            lw["bs0"], tm=tm_s, sample=True)
        sgb, kv, q, gn = _attn_in_sample_call(xs, lw["wgb"], lw["wkv"], lw["wq"], lw["wgn"])
        qh = q.reshape(n_dec, KV_HEADS, Q_PER_KV, HEAD_DIM)
        zq = jnp.zeros_like(qh[:, 0])
        qp = jnp.concatenate([jnp.concatenate([qh[:, 0], zq], -1), jnp.concatenate([zq, qh[:, 1]], -1)], 1).astype(BF)
        gates = jnp.pad(gn[:, :3 * N_HEADS].reshape(n_dec, N_HEADS, 3), ((0, 0), (0, 0), (0, LANES - 3)))
        o16 = _nsa_sample_call(pt, qp, kv[:, None, :], gates, cwin, cache, lw["cw"], ov_s, expand, layer=l,
                               n_past=n_past)
        o = jnp.concatenate([o16[:, :Q_PER_KV, :HEAD_DIM], o16[:, Q_PER_KV:, HEAD_DIM:]], 1).reshape(n_dec, D_ATTN)
        xr, cnt = _merge_call(xs, ya, sgb, o, lw["wpb"], lw["wo"], lw["ln1_g"], lw["ln1_b"], wrt, br, tri_s,
                              tm=tm_s, alpha=alpha, o_transposed=False)
        xs = _moe(xr, cnt, lw, tm=tm_s, tmx=tmx_s, alpha=alpha)
        kvr = kv.reshape(n_dec, 1, 6, KV_HEADS, HEAD_DIM)
        kv_s.append(kvr[:, :, :N_KV_SLOTS])
        win_s.append(jnp.concatenate([cache_win[:, l], kvr[:, :, N_KV_SLOTS:]], axis=1)[:, -wb:])
        gv_s.append(vn[:, None, :])
    kv_all = jnp.stack(kv_p, axis=0).reshape(depth, batch, seq, 6, KV_HEADS, HEAD_DIM)
    kv_rows_prompt = kv_all[:, :, :, :N_KV_SLOTS].transpose(1, 0, 2, 3, 4, 5)
    win_prompt = kv_all[:, :, seq - min(WINDOW, seq):, N_KV_SLOTS:].transpose(1, 0, 2, 3, 4, 5)
    return (xp.reshape(batch, seq, D_MODEL), xs.reshape(n_dec, 1, D_MODEL), kv_rows_prompt, win_prompt,
            jnp.stack(kv_s, axis=1), jnp.stack(win_s, axis=1), jnp.stack(gv_s, axis=1))
```
